```python
import jax, jax.numpy as jnp
from jax import lax
import numpy as np

D_MODEL = 2048
BATCH = 2
SEQ = 8192
DEPTH = 1

N_META = 16
EPS = 1e-6
HG_HEADS = 8
HG_KDIM = 128
HG_VDIM = 128
HG_WIDTH_K = HG_HEADS * HG_KDIM
HG_WIDTH_V = HG_HEADS * HG_VDIM
HG_CHUNK = 64
AT_HEADS = 8
AT_HDIM = 128
AT_WIDTH = AT_HEADS * AT_HDIM
IDX_HEADS = 16
IDX_HDIM = 64
IDX_TOPK_MAX = 256
Q_BLOCK = 128
D_FF = 5632
CONV_W = 3
IN_SIZES = (HG_WIDTH_K, HG_WIDTH_K, HG_WIDTH_V, HG_WIDTH_V,
            AT_WIDTH, AT_WIDTH, AT_WIDTH,
            IDX_HEADS * IDX_HDIM, IDX_HDIM, IDX_HEADS,
            D_MODEL, D_MODEL)
D_IN = sum(IN_SIZES)

kernel_name = "hgrn2_dsa_gated_hybrid_block"


def rmsnorm(x, w):
    xf = x.astype(jnp.float32)
    y = xf * lax.rsqrt(jnp.mean(xf * xf, axis=-1, keepdims=True) + EPS)
    return (y * w.astype(jnp.float32)).astype(x.dtype)


def split_cols(z, sizes):
    offs = []
    acc = 0
    for s in sizes[:-1]:
        acc += s
        offs.append(acc)
    return jnp.split(z, offs, axis=-1)


def hgrn2_mix(q_raw, f_raw, i_raw, g_raw, lb, norm_w):
    B, T, _ = q_raw.shape
    f32 = jnp.float32
    H, dk, dv, C = HG_HEADS, HG_KDIM, HG_VDIM, HG_CHUNK
    q = jax.nn.silu(q_raw.astype(f32))
    z = f_raw.astype(f32)
    lbf = lb.astype(f32)
    lf = jnp.log(lbf + (1.0 - lbf) * jax.nn.sigmoid(z))
    kc = (1.0 - lbf) * jax.nn.sigmoid(-z)
    v = i_raw.astype(f32)
    pad = (-N_META) % C
    Tp = T + pad
    N = Tp // C

    def to_chunks(a, d):
        a = jnp.pad(a, ((0, 0), (pad, 0), (0, 0)))
        return a.reshape(B, N, C, H, d).transpose(1, 0, 3, 2, 4)

    mask = jnp.tril(jnp.ones((C, C), dtype=bool))

    def step(S, inp):
        qc, kcc, vc, lfc = inp
        b = jnp.cumsum(lfc, axis=2)
        o_inter = jnp.einsum('bhtd,bhde->bhte', qc * jnp.exp(b), S)
        diff = b[:, :, :, None, :] - b[:, :, None, :, :]
        decay = jnp.exp(jnp.where(mask[None, None, :, :, None], diff, -jnp.inf))
        A = jnp.einsum('bhtd,bhtsd,bhsd->bhts', qc, decay, kcc)
        o = o_inter + jnp.einsum('bhts,bhse->bhte', A, vc)
        b_last = b[:, :, -1:, :]
        S = jnp.exp(b_last[:, :, 0, :])[..., None] * S + jnp.einsum(
            'bhsd,bhse->bhde', kcc * jnp.exp(b_last - b), vc)
        return S, o

    S0 = jnp.zeros((B, H, dk, dv), f32)
    _, o = lax.scan(step, S0, (to_chunks(q, dk), to_chunks(kc, dk),
                               to_chunks(v, dv), to_chunks(lf, dk)))
    o = o.transpose(1, 0, 3, 2, 4).reshape(B, Tp, H, dv)[:, pad:]
    o = rmsnorm(o, norm_w) * jax.nn.silu(g_raw.astype(f32).reshape(B, T, H, dv))
    return o.reshape(B, T, H * dv).astype(q_raw.dtype)


def causal_dwconv(u, w, b):
    Cn = u.shape[-1]
    y = lax.conv_general_dilated(u, w[:, None, :].astype(u.dtype), window_strides=(1,),
                                 padding=[(CONV_W - 1, 0)],
                                 dimension_numbers=('NWC', 'WIO', 'NWC'),
                                 feature_group_count=Cn)
    return y + b.astype(u.dtype)


def dsa_mix(q, k, v, qi, ki, wi, topk):
    B, T, _ = q.shape
    f32 = jnp.float32
    q = q.reshape(B, T, AT_HEADS, AT_HDIM)
    k = k.reshape(B, T, AT_HEADS, AT_HDIM)
    v = v.reshape(B, T, AT_HEADS, AT_HDIM)
    qi = qi.reshape(B, T, IDX_HEADS, IDX_HDIM)
    wi = wi * (IDX_HEADS ** -0.5)
    nb = -(-T // Q_BLOCK)
    Tq = nb * Q_BLOCK

    def blocks(a):
        a = jnp.pad(a, [(0, 0), (0, Tq - T)] + [(0, 0)] * (a.ndim - 2))
        return a.reshape((B, nb, Q_BLOCK) + a.shape[2:]).swapaxes(0, 1)

    key_pos = jnp.arange(T)
    gather = jax.vmap(lambda a, i: a[i])

    def block(inp):
        bi, qb, qib, wib = inp
        qpos = bi * Q_BLOCK + jnp.arange(Q_BLOCK)
        causal = key_pos[None, :] <= qpos[:, None]
        s_idx = jnp.einsum('bqhd,bsd->bqhs', qib, ki) * (IDX_HDIM ** -0.5)
        score = jnp.einsum('bqh,bqhs->bqs', wib, jax.nn.relu(s_idx)).astype(f32)
        score = jnp.where(causal[None], score, -jnp.inf)
        _, sel = lax.top_k(score, topk)
        valid = sel <= qpos[None, :, None]
        k_sel = gather(k, sel)
        v_sel = gather(v, sel)
        logits = jnp.einsum('bqhd,bqkhd->bqhk', qb, k_sel).astype(f32) * (AT_HDIM ** -0.5)
        logits = jnp.where(valid[:, :, None, :], logits, -jnp.inf)
        p = jax.nn.softmax(logits, axis=-1).astype(v.dtype)
        return jnp.einsum('bqhk,bqkhd->bqhd', p, v_sel)

    out = lax.map(block, (jnp.arange(nb), blocks(q), blocks(qi), blocks(wi)))
    return out.swapaxes(0, 1).reshape(B, Tq, AT_WIDTH)[:, :T]


def setup_inputs(seed: int = 0) -> dict:
    key = jax.random.key(seed)
    ks = jax.random.split(key, 20)
    nrm = jax.random.normal
    f32 = jnp.float32

    def gain(k, shape):
        return 1.0 + 0.02 * nrm(k, shape, f32)

    return {
        "x": nrm(ks[0], (BATCH, SEQ, D_MODEL), f32),
        "meta_tokens": nrm(ks[1], (N_META, D_MODEL), f32),
        "attn_norm_w": gain(ks[2], (DEPTH, D_MODEL)),
        "w_in": nrm(ks[3], (DEPTH, D_MODEL, D_IN), f32) * D_MODEL ** -0.5,
        "hgrn_lb_logits": 0.5 * nrm(ks[4], (DEPTH + 1, HG_WIDTH_K), f32),
        "hgrn_norm_w": gain(ks[5], (DEPTH, HG_VDIM)),
        "idx_k_norm_w": gain(ks[6], (DEPTH, IDX_HDIM)),
        "w_branch_hgrn": nrm(ks[7], (DEPTH, HG_WIDTH_V, D_MODEL), f32) * HG_WIDTH_V ** -0.5,
        "w_branch_dsa": nrm(ks[8], (DEPTH, AT_WIDTH, D_MODEL), f32) * AT_WIDTH ** -0.5,
        "w_out": nrm(ks[9], (DEPTH, D_MODEL, D_MODEL), f32) * D_MODEL ** -0.5,
        "ffn_norm_w": gain(ks[10], (DEPTH, D_MODEL)),
        "w_ffn_gate": nrm(ks[11], (DEPTH, D_MODEL, D_FF), f32) * D_MODEL ** -0.5,
        "w_ffn_up": nrm(ks[12], (DEPTH, D_MODEL, D_FF), f32) * D_MODEL ** -0.5,
        "ffn_conv_w": nrm(ks[13], (DEPTH, CONV_W, D_FF), f32) * CONV_W ** -0.5,
        "ffn_conv_b": 0.01 * nrm(ks[14], (DEPTH, D_FF), f32),
        "w_ffn_down": nrm(ks[15], (DEPTH, D_FF, D_MODEL), f32) * D_FF ** -0.5,
        "final_norm_w": gain(ks[16], (D_MODEL,)),
    }


def reference(x, meta_tokens, attn_norm_w, w_in, hgrn_lb_logits, hgrn_norm_w, idx_k_norm_w,
              w_branch_hgrn, w_branch_dsa, w_out, ffn_norm_w, w_ffn_gate, w_ffn_up,
              ffn_conv_w, ffn_conv_b, w_ffn_down, final_norm_w):
    B, L, _ = x.shape
    topk = min(IDX_TOPK_MAX, L // 4)
    meta = jnp.broadcast_to(meta_tokens.astype(x.dtype)[None], (B, N_META, D_MODEL))
    h = jnp.concatenate([meta, x], axis=1)
    lbs = jnp.cumsum(jax.nn.softmax(hgrn_lb_logits.astype(jnp.float32), axis=0), axis=0)
    for l in range(DEPTH):
        u = rmsnorm(h, attn_norm_w[l])
        z = u @ w_in[l]
        hq, hf, hi, hg, aq, ak, av, iq, ik, iw, ga, gb = split_cols(z, IN_SIZES)
        y_h = hgrn2_mix(hq, hf, hi, hg, lbs[l], hgrn_norm_w[l])
        y_a = dsa_mix(aq, ak, av, iq, rmsnorm(ik, idx_k_norm_w[l]), iw, topk)
        m = (jax.nn.sigmoid(ga) * (y_h @ w_branch_hgrn[l])
             + jax.nn.sigmoid(gb) * (y_a @ w_branch_dsa[l]))
        h = h + m @ w_out[l]
        u = rmsnorm(h, ffn_norm_w[l])
        a = causal_dwconv(u @ w_ffn_gate[l], ffn_conv_w[l], ffn_conv_b[l])
        h = h + (jax.nn.silu(a) * (u @ w_ffn_up[l])) @ w_ffn_down[l]
    return rmsnorm(h, final_norm_w)[:, N_META:]
```

```python
import functools
import math

import jax
import jax.numpy as jnp
from jax import lax
from jax.experimental import pallas as pl
from jax.experimental.pallas import tpu as pltpu

N_META = 16
EPS = 1e-6
HG_HEADS = 8
HG_DIM = 128
AT_HEADS = 8
AT_HDIM = 128
IDX_HEADS = 16
IDX_HDIM = 64
IDX_TOPK_MAX = 256
CONV_W = 3

LANES = 128
SEQ_OFF = 512
META0 = SEQ_OFF - N_META
NEG_BIG = -1e30
INT_MIN = -(2 ** 31)

HG_CHUNK = 128
HG_DIAG = 8
HG_STEP = 512
DSA_TQ = 256
DSA_KC = 512
ATT_GROUPS = 2
VMEM_LIMIT = 56 * 1024 * 1024

F32 = jnp.float32
BF16 = jnp.bfloat16
NT_DIMS = (((1,), (1,)), ((), ()))
TN_DIMS = (((0,), (0,)), ((), ()))


def _cparams(sem):
    return pltpu.CompilerParams(dimension_semantics=sem, vmem_limit_bytes=VMEM_LIMIT)


def _rms_rows(x, w):
    ms = jnp.mean(x * x, axis=-1, keepdims=True)
    return x * lax.rsqrt(ms + EPS) * w


def _norm_matmul_kernel(h_ref, nw_ref, w_ref, o_ref, u_ref):
    @pl.when(pl.program_id(1) == 0)
    def _():
        u_ref[...] = _rms_rows(h_ref[...], nw_ref[...]).astype(u_ref.dtype)

    o_ref[...] = jnp.dot(u_ref[...], w_ref[...], preferred_element_type=F32).astype(o_ref.dtype)


def _norm_matmul(h, nw, w, tm, tn):
    m, d = h.shape
    n = w.shape[1]
    return pl.pallas_call(
        _norm_matmul_kernel,
        grid=(m // tm, n // tn),
        in_specs=[pl.BlockSpec((tm, d), lambda i, j: (i, 0)),
                  pl.BlockSpec((1, d), lambda i, j: (0, 0)),
                  pl.BlockSpec((d, tn), lambda i, j: (0, j))],
        out_specs=pl.BlockSpec((tm, tn), lambda i, j: (i, j)),
        out_shape=jax.ShapeDtypeStruct((m, n), BF16),
        scratch_shapes=[pltpu.VMEM((tm, d), BF16)],
        compiler_params=_cparams(("parallel", "arbitrary")),
        name="in_proj",
    )(h, nw, w)


def _idx_small_kernel(h_ref, nw_ref, w_ref, knw_ref, zs_ref, kt_ref, kb_ref):
    u = _rms_rows(h_ref[...], nw_ref[...]).astype(BF16)
    zs = jnp.dot(u, w_ref[...], preferred_element_type=F32)
    lane = lax.broadcasted_iota(jnp.int32, zs.shape, 1)
    ik = jnp.where(lane < IDX_HDIM, zs, 0.0)
    ms = jnp.sum(ik * ik, axis=-1, keepdims=True) / IDX_HDIM
    kn = ik * lax.rsqrt(ms + EPS) * knw_ref[...]
    zs_ref[...] = zs
    kt_ref[...] = kn.astype(BF16)
    kb_ref[...] = pltpu.roll(kn, IDX_HDIM, axis=1).astype(BF16)


def _idx_small(h, nw, w, knw, tm):
    m, d = h.shape
    row = lambda i: (i, 0)
    fixed = lambda i: (0, 0)
    return pl.pallas_call(
        _idx_small_kernel,
        grid=(m // tm,),
        in_specs=[pl.BlockSpec((tm, d), row), pl.BlockSpec((1, d), fixed),
                  pl.BlockSpec((d, LANES), fixed), pl.BlockSpec((1, LANES), fixed)],
        out_specs=[pl.BlockSpec((tm, LANES), row)] * 3,
        out_shape=[jax.ShapeDtypeStruct((m, LANES), F32),
                   jax.ShapeDtypeStruct((m, LANES), BF16),
                   jax.ShapeDtypeStruct((m, LANES), BF16)],
        compiler_params=_cparams(("parallel",)),
        name="idx_proj",
    )(h, nw, w, knw)


def _hgrn_kernel(q_ref, f_ref, i_ref, g_ref, lb_ref, nw_ref, o_ref, st_ref):
    c = HG_CHUNK

    @pl.when(pl.program_id(2) == 0)
    def _():
        st_ref[...] = jnp.zeros_like(st_ref)

    lb = lb_ref[...]
    oml = 1.0 - lb
    row = lax.broadcasted_iota(jnp.int32, (c, 1), 0)
    col = lax.broadcasted_iota(jnp.int32, (1, c), 1)

    def chunk(ci, carry):
        r0 = pl.multiple_of(ci * c, c)
        rows = pl.ds(r0, c)
        qz = q_ref[rows, :].astype(F32)
        fz = f_ref[rows, :].astype(F32)
        v = i_ref[rows, :]
        gz = g_ref[rows, :].astype(F32)

        q = qz * jax.nn.sigmoid(qz)
        f = lb + oml * jax.nn.sigmoid(fz)
        kc = oml * jax.nn.sigmoid(-fz)
        b = jnp.log(f)
        s = 1
        while s < c:
            b = b + jnp.where(row >= s, pltpu.roll(b, s, axis=0), 0.0)
            s *= 2

        a = jnp.zeros((c, c), F32)
        m = c // 2
        while m >= HG_DIAG:
            piv = jnp.concatenate(
                [jnp.broadcast_to(b[blk * 2 * m + m - 1: blk * 2 * m + m, :], (2 * m, HG_DIM))
                 for blk in range(c // (2 * m))], axis=0)
            x = jnp.exp(-jnp.abs(b - piv))
            upper = (row & m) != 0
            qm = (jnp.where(upper, q, 0.0) * x).astype(BF16)
            km = (jnp.where(upper, 0.0, kc) * x).astype(BF16)
            am = lax.dot_general(qm, km, NT_DIMS, preferred_element_type=F32)
            same = (row // (2 * m)) == (col // (2 * m))
            a = a + jnp.where(same, am, 0.0)
            m //= 2

        gd = None
        for d in range(HG_DIAG):
            if d == 0:
                xd = q * kc
            else:
                fd = f if d == 1 else pltpu.roll(f, d - 1, axis=0)
                gd = fd if gd is None else gd * fd
                xd = q * pltpu.roll(kc, d, axis=0) * gd
            rd = jnp.sum(xd, axis=-1, keepdims=True)
            rd = jnp.where((row & (HG_DIAG - 1)) >= d, rd, 0.0)
            a = a + jnp.where(col == row - d, rd, 0.0)

        st = st_ref[...]
        qd = (q * jnp.exp(b)).astype(BF16)
        o = lax.dot_general(qd, st.astype(BF16), NT_DIMS, preferred_element_type=F32)
        o = o + jnp.dot(a.astype(BF16), v, preferred_element_type=F32)

        b_last = b[c - 1:c, :]
        kd = (kc * jnp.exp(b_last - b)).astype(BF16)
        st_ref[...] = st * jnp.exp(b_last) + lax.dot_general(v, kd, TN_DIMS, preferred_element_type=F32)

        y = _rms_rows(o, nw_ref[...]) * (gz * jax.nn.sigmoid(gz))
        o_ref[rows, :] = y.astype(o_ref.dtype)
        return carry

    lax.fori_loop(0, q_ref.shape[0] // c, chunk, 0)


def _hgrn(z3, lb, nw):
    bsz, tp, _ = z3.shape

    def zspec(group):
        return pl.BlockSpec((None, HG_STEP, HG_DIM), lambda b, h, s: (b, s, group * HG_HEADS + h))

    return pl.pallas_call(
        _hgrn_kernel,
        grid=(bsz, HG_HEADS, tp // HG_STEP),
        in_specs=[zspec(0), zspec(1), zspec(2), zspec(3),
                  pl.BlockSpec((1, HG_DIM), lambda b, h, s: (0, h)),
                  pl.BlockSpec((1, HG_DIM), lambda b, h, s: (0, 0))],
        out_specs=pl.BlockSpec((None, HG_STEP, HG_DIM), lambda b, h, s: (b, s, h)),
        out_shape=jax.ShapeDtypeStruct((bsz, tp, HG_HEADS * HG_DIM), BF16),
        scratch_shapes=[pltpu.VMEM((HG_DIM, HG_DIM), F32)],
        compiler_params=_cparams(("parallel", "parallel", "arbitrary")),
        name="hgrn2",
    )(z3, z3, z3, z3, lb, nw)


def _dsa_index_kernel(iq_ref, wt_ref, kt_ref, kb_ref, bias_ref, keys_ref, acc_ref, *, topk):
    tq = iq_ref.shape[0]
    kc = DSA_KC
    tp = kt_ref.shape[0]
    q0 = pl.program_id(1) * tq
    nchunks = (q0 + tq - 1) // kc + 1
    qpos = q0 + lax.broadcasted_iota(jnp.int32, (1, tq), 1)
    int_min = jnp.int32(INT_MIN)

    def score_chunk(ci, carry):
        k0 = pl.multiple_of(ci * kc, kc)
        kt = kt_ref[pl.ds(k0, kc), :]
        kb = kb_ref[pl.ds(k0, kc), :]
        for p in range(IDX_HEADS // 2):
            qp = iq_ref[:, p * LANES:(p + 1) * LANES]
            st = lax.dot_general(kt, qp, NT_DIMS, preferred_element_type=F32)
            sb = lax.dot_general(kb, qp, NT_DIMS, preferred_element_type=F32)
            part = (wt_ref[2 * p:2 * p + 1, :] * jnp.maximum(st, 0.0)
                    + wt_ref[2 * p + 1:2 * p + 2, :] * jnp.maximum(sb, 0.0))
            if p == 0:
                acc_ref[...] = part
            else:
                acc_ref[...] += part
        score = acc_ref[...] + 0.0
        bits = lax.bitcast_convert_type(score, jnp.int32)
        key = bits ^ ((bits >> 31) & jnp.int32(0x7FFFFFFF))
        kpos = k0 + lax.broadcasted_iota(jnp.int32, (kc, 1), 0)
        valid = (kpos >= META0) & (kpos <= qpos)
        keys_ref[pl.ds(k0, kc), :] = jnp.where(valid, key, int_min)
        return carry

    lax.fori_loop(0, nchunks, score_chunk, 0)

    def bit_pass(i, u):
        bit = lax.shift_left(jnp.int32(1), 31 - i)
        cand_u = u | bit
        cand = cand_u ^ int_min

        def count_chunk(ci, cnt):
            k0 = pl.multiple_of(ci * kc, kc)
            ks = keys_ref[pl.ds(k0, kc), :]
            return cnt + jnp.sum(jnp.where(ks >= cand, 1.0, 0.0), axis=0, keepdims=True)

        cnt = lax.fori_loop(0, nchunks, count_chunk, jnp.zeros((1, tq), F32))
        return jnp.where(cnt >= topk, cand_u, u)

    u = lax.fori_loop(0, 32, bit_pass, jnp.zeros((1, tq), jnp.int32))
    thr = jnp.where(u == 0, jnp.int32(1), u) ^ int_min

    def write_chunk(ci, carry):
        k0 = pl.multiple_of(ci * kc, kc)
        ks = keys_ref[pl.ds(k0, kc), :]
        bias_ref[pl.ds(k0, kc), :] = jnp.where(ks >= thr, 0.0, NEG_BIG).astype(bias_ref.dtype)
        return carry

    lax.fori_loop(0, nchunks, write_chunk, 0)

    def fill_chunk(ci, carry):
        k0 = pl.multiple_of(ci * kc, kc)
        bias_ref[pl.ds(k0, kc), :] = jnp.full((kc, tq), NEG_BIG, bias_ref.dtype)
        return carry

    lax.fori_loop(nchunks, tp // kc, fill_chunk, 0)


def _dsa_index(z3, wt, kt, kb, topk):
    bsz, tp, _ = z3.shape
    iq_col = (HG_HEADS * HG_DIM * 4 + AT_HEADS * AT_HDIM * 3) // (IDX_HEADS * IDX_HDIM)
    return pl.pallas_call(
        functools.partial(_dsa_index_kernel, topk=topk),
        grid=(bsz, tp // DSA_TQ),
        in_specs=[pl.BlockSpec((None, DSA_TQ, IDX_HEADS * IDX_HDIM), lambda b, i: (b, i, iq_col)),
                  pl.BlockSpec((None, IDX_HEADS, DSA_TQ), lambda b, i: (b, 0, i)),
                  pl.BlockSpec((None, tp, LANES), lambda b, i: (b, 0, 0)),
                  pl.BlockSpec((None, tp, LANES), lambda b, i: (b, 0, 0))],
        out_specs=pl.BlockSpec((None, tp, DSA_TQ), lambda b, i: (b, 0, i)),
        out_shape=jax.ShapeDtypeStruct((bsz, tp, tp), BF16),
        scratch_shapes=[pltpu.VMEM((tp, DSA_TQ), jnp.int32), pltpu.VMEM((DSA_KC, DSA_TQ), F32)],
        compiler_params=_cparams(("parallel", "arbitrary")),
        name="dsa_index",
    )(z3, wt, kt, kb)


def _dsa_attn_kernel(aq_ref, k_ref, vt_ref, bias_ref, o_ref, qs_ref, m_ref, l_ref, acc_ref):
    tq = aq_ref.shape[0]
    kc = DSA_KC
    heads = aq_ref.shape[1] // AT_HDIM
    q0 = pl.program_id(2) * tq
    nchunks = (q0 + tq - 1) // kc + 1

    qs_ref[...] = (aq_ref[...].astype(F32) * (AT_HDIM ** -0.5 * math.log2(math.e))).astype(qs_ref.dtype)
    m_ref[...] = jnp.full(m_ref.shape, NEG_BIG, F32)
    l_ref[...] = jnp.zeros(l_ref.shape, F32)
    acc_ref[...] = jnp.zeros(acc_ref.shape, F32)

    def chunk(ci, carry):
        k0 = pl.multiple_of(ci * kc, kc)
        bias = bias_ref[pl.ds(k0, kc), :].astype(F32)
        for h in range(heads):
            hs = slice(h * AT_HDIM, (h + 1) * AT_HDIM)
            lg = lax.dot_general(k_ref[pl.ds(k0, kc), hs], qs_ref[:, hs], NT_DIMS,
                                 preferred_element_type=F32) + bias
            m_old = m_ref[h:h + 1, :]
            m_new = jnp.maximum(m_old, jnp.max(lg, axis=0, keepdims=True))
            alpha = jnp.exp2(m_old - m_new)
            p = jnp.exp2(lg - m_new)
            l_ref[h:h + 1, :] = alpha * l_ref[h:h + 1, :] + jnp.sum(p, axis=0, keepdims=True)
            pv = jnp.dot(vt_ref[ci, hs, :], p.astype(BF16), preferred_element_type=F32)
            acc_ref[h] = alpha * acc_ref[h] + pv
            m_ref[h:h + 1, :] = m_new
        return carry

    lax.fori_loop(0, nchunks, chunk, 0)

    live = (q0 + lax.broadcasted_iota(jnp.int32, (1, tq), 1)) >= META0
    for h in range(heads):
        out_t = jnp.where(live, acc_ref[h] / l_ref[h:h + 1, :], 0.0)
        o_ref[:, h * AT_HDIM:(h + 1) * AT_HDIM] = out_t.T.astype(o_ref.dtype)


def _dsa_attn(z3, vt, bias):
    bsz, tp, _ = z3.shape
    gw = AT_HEADS * AT_HDIM // ATT_GROUPS
    aq_col = HG_HEADS * HG_DIM * 4 // gw
    ak_col = aq_col + ATT_GROUPS
    hpg = AT_HEADS // ATT_GROUPS
    return pl.pallas_call(
        _dsa_attn_kernel,
        grid=(bsz, ATT_GROUPS, tp // DSA_TQ),
        in_specs=[pl.BlockSpec((None, DSA_TQ, gw), lambda b, g, i: (b, i, aq_col + g)),
                  pl.BlockSpec((None, tp, gw), lambda b, g, i: (b, 0, ak_col + g)),
                  pl.BlockSpec((None, tp // DSA_KC, gw, DSA_KC), lambda b, g, i: (b, 0, g, 0)),
                  pl.BlockSpec((None, tp, DSA_TQ), lambda b, g, i: (b, 0, i))],
        out_specs=pl.BlockSpec((None, DSA_TQ, gw), lambda b, g, i: (b, i, g)),
        out_shape=jax.ShapeDtypeStruct((bsz, tp, AT_HEADS * AT_HDIM), BF16),
        scratch_shapes=[pltpu.VMEM((DSA_TQ, gw), BF16),
                        pltpu.VMEM((hpg, DSA_TQ), F32),
                        pltpu.VMEM((hpg, DSA_TQ), F32),
                        pltpu.VMEM((hpg, AT_HDIM, DSA_TQ), F32)],
        compiler_params=_cparams(("parallel", "parallel", "arbitrary")),
        name="dsa_attn",
    )(z3, z3, vt, bias)


def _merge_kernel(yh_ref, ya_ref, ga_ref, gb_ref, wh_ref, wa_ref, o_ref):
    mh = jnp.dot(yh_ref[...], wh_ref[...], preferred_element_type=F32)
    ma = jnp.dot(ya_ref[...], wa_ref[...], preferred_element_type=F32)
    o = jax.nn.sigmoid(ga_ref[...].astype(F32)) * mh + jax.nn.sigmoid(gb_ref[...].astype(F32)) * ma
    o_ref[...] = o.astype(o_ref.dtype)


def _merge(yh, ya, z, wh, wa, tm, tn):
    m, kdim = yh.shape
    n = wh.shape[1]
    ga_col = (z.shape[1] - 2 * n) // tn
    gb_col = (z.shape[1] - n) // tn
    return pl.pallas_call(
        _merge_kernel,
        grid=(m // tm, n // tn),
        in_specs=[pl.BlockSpec((tm, kdim), lambda i, j: (i, 0)),
                  pl.BlockSpec((tm, kdim), lambda i, j: (i, 0)),
                  pl.BlockSpec((tm, tn), lambda i, j: (i, ga_col + j)),
                  pl.BlockSpec((tm, tn), lambda i, j: (i, gb_col + j)),
                  pl.BlockSpec((kdim, tn), lambda i, j: (0, j)),
                  pl.BlockSpec((kdim, tn), lambda i, j: (0, j))],
        out_specs=pl.BlockSpec((tm, tn), lambda i, j: (i, j)),
        out_shape=jax.ShapeDtypeStruct((m, n), BF16),
        compiler_params=_cparams(("parallel", "arbitrary")),
        name="branch_merge",
    )(yh, ya, z, z, wh, wa)


def _out_proj_kernel(h_ref, m_ref, w_ref, nw_ref, h1_ref, u2_ref):
    h1 = h_ref[...] + jnp.dot(m_ref[...], w_ref[...], preferred_element_type=F32)
    h1_ref[...] = h1
    u2_ref[...] = _rms_rows(h1, nw_ref[...]).astype(u2_ref.dtype)


def _out_proj(h, mm, w, nw, tm):
    m, d = h.shape
    row = lambda i: (i, 0)
    fixed = lambda i: (0, 0)
    return pl.pallas_call(
        _out_proj_kernel,
        grid=(m // tm,),
        in_specs=[pl.BlockSpec((tm, d), row), pl.BlockSpec((tm, d), row),
                  pl.BlockSpec((d, d), fixed), pl.BlockSpec((1, d), fixed)],
        out_specs=[pl.BlockSpec((tm, d), row), pl.BlockSpec((tm, d), row)],
        out_shape=[jax.ShapeDtypeStruct((m, d), F32), jax.ShapeDtypeStruct((m, d), BF16)],
        compiler_params=_cparams(("parallel",)),
        name="out_proj",
    )(h, mm, w, nw)


def _ffn_up_kernel(u_ref, wg_ref, wu_ref, cw_ref, cb_ref, o_ref, tail_ref, *, blocks_per_seq):
    tm = u_ref.shape[0]

    @pl.when(pl.program_id(1) % blocks_per_seq == 0)
    def _():
        tail_ref[...] = jnp.zeros_like(tail_ref)

    u = u_ref[...]
    g0 = jnp.dot(u, wg_ref[...], preferred_element_type=F32)
    up = jnp.dot(u, wu_ref[...], preferred_element_type=F32)
    row = lax.broadcasted_iota(jnp.int32, (tm, 1), 0)
    prev1 = tail_ref[7:8, :]
    prev2 = tail_ref[6:7, :]
    g1 = jnp.where(row == 0, prev1, pltpu.roll(g0, 1, axis=0))
    g2 = jnp.where(row == 0, prev2, jnp.where(row == 1, prev1, pltpu.roll(g0, 2, axis=0)))
    tail_ref[...] = g0[tm - 8:tm, :]
    a = cw_ref[0:1, :] * g2 + cw_ref[1:2, :] * g1 + cw_ref[2:3, :] * g0 + cb_ref[...]
    o_ref[...] = (a * jax.nn.sigmoid(a) * up).astype(o_ref.dtype)


def _ffn_up(u2, wg, wu, cw, cb, tm, tn, blocks_per_seq):
    m, d = u2.shape
    n = wg.shape[1]
    return pl.pallas_call(
        functools.partial(_ffn_up_kernel, blocks_per_seq=blocks_per_seq),
        grid=(n // tn, m // tm),
        in_specs=[pl.BlockSpec((tm, d), lambda j, i: (i, 0)),
                  pl.BlockSpec((d, tn), lambda j, i: (0, j)),
                  pl.BlockSpec((d, tn), lambda j, i: (0, j)),
                  pl.BlockSpec((8, tn), lambda j, i: (0, j)),
                  pl.BlockSpec((1, tn), lambda j, i: (0, j))],
        out_specs=pl.BlockSpec((tm, tn), lambda j, i: (i, j)),
        out_shape=jax.ShapeDtypeStruct((m, n), BF16),
        scratch_shapes=[pltpu.VMEM((8, tn), F32)],
        compiler_params=_cparams(("parallel", "arbitrary")),
        name="ffn_up",
    )(u2, wg, wu, cw, cb)


def _ffn_down_kernel(g_ref, wd_ref, h1_ref, nw_ref, o_ref, acc_ref):
    k = pl.program_id(1)

    @pl.when(k == 0)
    def _():
        acc_ref[...] = h1_ref[...]

    acc_ref[...] += jnp.dot(g_ref[...], wd_ref[...], preferred_element_type=F32)

    @pl.when(k == pl.num_programs(1) - 1)
    def _():
        o_ref[...] = _rms_rows(acc_ref[...], nw_ref[...])


def _ffn_down(gact, wd, h1, nw, bsz, seq, tp, tm, tk):
    m, kdim = gact.shape
    d = wd.shape[1]
    per_seq = seq // tm
    off = SEQ_OFF // tm

    def row_in(i, k):
        return (i // per_seq) * (tp // tm) + off + i % per_seq

    return pl.pallas_call(
        _ffn_down_kernel,
        grid=(bsz * per_seq, kdim // tk),
        in_specs=[pl.BlockSpec((tm, tk), lambda i, k: (row_in(i, k), k)),
                  pl.BlockSpec((tk, d), lambda i, k: (k, 0)),
                  pl.BlockSpec((tm, d), lambda i, k: (row_in(i, k), 0)),
                  pl.BlockSpec((1, d), lambda i, k: (0, 0))],
        out_specs=pl.BlockSpec((tm, d), lambda i, k: (i, 0)),
        out_shape=jax.ShapeDtypeStruct((bsz * seq, d), F32),
        scratch_shapes=[pltpu.VMEM((tm, d), F32)],
        compiler_params=_cparams(("parallel", "arbitrary")),
        name="ffn_down",
    )(gact, wd, h1, nw)


def kernel(x, meta_tokens, attn_norm_w, w_in, hgrn_lb_logits, hgrn_norm_w, idx_k_norm_w, w_branch_hgrn,
           w_branch_dsa, w_out, ffn_norm_w, w_ffn_gate, w_ffn_up, ffn_conv_w, ffn_conv_b, w_ffn_down,
           final_norm_w):
    bsz, seq, d = x.shape
    depth = w_in.shape[0]
    topk = min(IDX_TOPK_MAX, seq // 4)
    tp = SEQ_OFF + seq
    m = bsz * tp
    hg_w = HG_HEADS * HG_DIM
    at_w = AT_HEADS * AT_HDIM
    n_main_lo = 4 * hg_w + 3 * at_w + IDX_HEADS * IDX_HDIM
    n_small = IDX_HDIM + IDX_HEADS

    meta = jnp.broadcast_to(meta_tokens.astype(x.dtype)[None], (bsz, N_META, d))
    h = jnp.concatenate([jnp.zeros((bsz, META0, d), x.dtype), meta, x], axis=1).reshape(m, d)
    lbs = jnp.cumsum(jax.nn.softmax(hgrn_lb_logits.astype(F32), axis=0), axis=0)

    assert depth == 1, "multi-layer stacks are not supported"
    out = None
    for l in range(depth):
        w_l = w_in[l]
        w_main = jnp.concatenate([w_l[:, :n_main_lo], w_l[:, n_main_lo + n_small:]], axis=1).astype(BF16)
        w_small = jnp.pad(w_l[:, n_main_lo:n_main_lo + n_small], ((0, 0), (0, LANES - n_small))).astype(BF16)
        knw = jnp.pad(idx_k_norm_w[l], (0, LANES - IDX_HDIM)).reshape(1, LANES)
        nw_attn = attn_norm_w[l].reshape(1, d)

        z = _norm_matmul(h, nw_attn, w_main, tm=1024, tn=1024)
        zs, kt, kb = _idx_small(h, nw_attn, w_small, knw, tm=1024)
        z3 = z.reshape(bsz, tp, z.shape[1])

        y_h = _hgrn(z3, lbs[l].reshape(1, hg_w), hgrn_norm_w[l].reshape(1, HG_DIM))

        wt = zs.reshape(bsz, tp, LANES)[:, :, IDX_HDIM:IDX_HDIM + IDX_HEADS]
        wt = jnp.swapaxes(wt * (IDX_HEADS ** -0.5 * IDX_HDIM ** -0.5), 1, 2)
        bias = _dsa_index(z3, wt, kt.reshape(bsz, tp, LANES), kb.reshape(bsz, tp, LANES), topk)
        av = z3[:, :, 4 * hg_w + 2 * at_w:4 * hg_w + 3 * at_w]
        vt = jnp.swapaxes(av.reshape(bsz, tp // DSA_KC, DSA_KC, at_w), 2, 3)
        y_a = _dsa_attn(z3, vt, bias)

        mm = _merge(y_h.reshape(m, hg_w), y_a.reshape(m, at_w), z,
                    w_branch_hgrn[l].astype(BF16), w_branch_dsa[l].astype(BF16), tm=1024, tn=1024)
        h1, u2 = _out_proj(h, mm, w_out[l].astype(BF16), ffn_norm_w[l].reshape(1, d), tm=512)

        cw = jnp.pad(ffn_conv_w[l], ((0, 8 - CONV_W), (0, 0)))
        gact = _ffn_up(u2, w_ffn_gate[l].astype(BF16), w_ffn_up[l].astype(BF16), cw,
                       ffn_conv_b[l].reshape(1, -1), tm=512, tn=512, blocks_per_seq=tp // 512)
        out = _ffn_down(gact, w_ffn_down[l].astype(BF16), h1, final_norm_w.reshape(1, d),
                        bsz, seq, tp, tm=512, tk=1408)
    return out.reshape(bsz, seq, d)
```

```python
import functools
import math

import jax
import jax.numpy as jnp
from jax import lax
from jax.experimental import pallas as pl
from jax.experimental.pallas import tpu as pltpu

N_META = 16
EPS = 1e-6
HG_HEADS = 8
HG_DIM = 128
AT_HEADS = 8
AT_HDIM = 128
IDX_HEADS = 16
IDX_HDIM = 64
IDX_TOPK_MAX = 256
CONV_W = 3

LANES = 128
SEQ_OFF = 512
META0 = SEQ_OFF - N_META
NEG_BIG = -1e30
INT_MIN = -(2 ** 31)

HG_CHUNK = 128
HG_DIAG = 8
HG_STEP = 512
DSA_TQ = 256
DSA_KC = 512
COUNT_ROWS = 64
HALF_OFF = 2 ** 15
ATT_GROUPS = 2
VMEM_LIMIT = 56 * 1024 * 1024

F32 = jnp.float32
BF16 = jnp.bfloat16
NT_DIMS = (((1,), (1,)), ((), ()))
TN_DIMS = (((0,), (0,)), ((), ()))


def _cparams(sem):
    return pltpu.CompilerParams(dimension_semantics=sem, vmem_limit_bytes=VMEM_LIMIT)


def _rms_rows(x, w):
    ms = jnp.mean(x * x, axis=-1, keepdims=True)
    return x * lax.rsqrt(ms + EPS) * w


def _rows_reduce(x, pair_op, final_reduce):
    rows = x.shape[0]
    while rows > 64:
        rows //= 2
        x = pair_op(x[:rows], x[rows:])
    return final_reduce(x, axis=0, keepdims=True)


def _x_block(r, bps):
    return (r // bps) * (bps - 1) + jnp.maximum(r % bps - 1, 0)


def _embed_kernel(x_ref, pre_ref, nw_ref, w_ref, knw_ref, u_ref, zs_ref, kt_ref, kb_ref, *, bps):
    is_prefix = pl.program_id(0) % bps == 0
    h = jnp.where(is_prefix, pre_ref[...], x_ref[...])
    u = _rms_rows(h, nw_ref[...]).astype(BF16)
    u_ref[...] = u
    zs = jnp.dot(u, w_ref[...], preferred_element_type=F32)
    lane = lax.broadcasted_iota(jnp.int32, zs.shape, 1)
    ik = jnp.where(lane < IDX_HDIM, zs, 0.0)
    ms = jnp.sum(ik * ik, axis=-1, keepdims=True) / IDX_HDIM
    kn = ik * lax.rsqrt(ms + EPS) * knw_ref[...]
    zs_ref[...] = zs
    kt_ref[...] = kn.astype(BF16)
    kb_ref[...] = pltpu.roll(kn, IDX_HDIM, axis=1).astype(BF16)


def _embed(x2, prefix, nw, w, knw, bsz, tp):
    d = x2.shape[1]
    tm = SEQ_OFF
    bps = tp // tm
    m = bsz * tp
    row = lambda r: (r, 0)
    fixed = lambda r: (0, 0)
    return pl.pallas_call(
        functools.partial(_embed_kernel, bps=bps),
        grid=(m // tm,),
        in_specs=[pl.BlockSpec((tm, d), lambda r: (_x_block(r, bps), 0)),
                  pl.BlockSpec((tm, d), fixed), pl.BlockSpec((1, d), fixed),
                  pl.BlockSpec((d, LANES), fixed), pl.BlockSpec((1, LANES), fixed)],
        out_specs=[pl.BlockSpec((tm, d), row)] + [pl.BlockSpec((tm, LANES), row)] * 3,
        out_shape=[jax.ShapeDtypeStruct((m, d), BF16),
                   jax.ShapeDtypeStruct((m, LANES), F32),
                   jax.ShapeDtypeStruct((m, LANES), BF16),
                   jax.ShapeDtypeStruct((m, LANES), BF16)],
        compiler_params=_cparams(("parallel",)),
        name="embed_norm",
    )(x2, prefix, nw, w, knw)


def _in_proj_kernel(u_ref, wlo_ref, whi_ref, o_ref, *, n_lo):
    j = pl.program_id(1)

    @pl.when(j < n_lo)
    def _():
        o_ref[...] = jnp.dot(u_ref[...], wlo_ref[...], preferred_element_type=F32).astype(o_ref.dtype)

    @pl.when(j >= n_lo)
    def _():
        o_ref[...] = jnp.dot(u_ref[...], whi_ref[...], preferred_element_type=F32).astype(o_ref.dtype)


def _in_proj(u, w_lo, w_hi, tm, tn):
    m, d = u.shape
    n_lo = w_lo.shape[1] // tn
    n_hi = w_hi.shape[1] // tn
    return pl.pallas_call(
        functools.partial(_in_proj_kernel, n_lo=n_lo),
        grid=(m // tm, n_lo + n_hi),
        in_specs=[pl.BlockSpec((tm, d), lambda i, j: (i, 0)),
                  pl.BlockSpec((d, tn), lambda i, j: (0, jnp.minimum(j, n_lo - 1))),
                  pl.BlockSpec((d, tn), lambda i, j: (0, jnp.maximum(j - n_lo, 0)))],
        out_specs=pl.BlockSpec((tm, tn), lambda i, j: (i, j)),
        out_shape=jax.ShapeDtypeStruct((m, (n_lo + n_hi) * tn), BF16),
        compiler_params=_cparams(("parallel", "arbitrary")),
        name="in_proj",
    )(u, w_lo, w_hi)


def _hgrn_kernel(q_ref, f_ref, i_ref, g_ref, lb_ref, nw_ref, o_ref, st_ref):
    c = HG_CHUNK

    @pl.when(pl.program_id(2) == 0)
    def _():
        st_ref[...] = jnp.zeros_like(st_ref)

    lb = lb_ref[...]
    oml = 1.0 - lb
    row = lax.broadcasted_iota(jnp.int32, (c, 1), 0)
    col = lax.broadcasted_iota(jnp.int32, (1, c), 1)

    def chunk(ci, carry):
        r0 = pl.multiple_of(ci * c, c)
        rows = pl.ds(r0, c)
        qz = q_ref[rows, :].astype(F32)
        fz = f_ref[rows, :].astype(F32)
        v = i_ref[rows, :]
        gz = g_ref[rows, :].astype(F32)

        q = qz * jax.nn.sigmoid(qz)
        f = lb + oml * jax.nn.sigmoid(fz)
        kc = oml * jax.nn.sigmoid(-fz)
        b = jnp.log(f)
        s = 1
        while s < c:
            b = b + jnp.where(row >= s, pltpu.roll(b, s, axis=0), 0.0)
            s *= 2

        a = jnp.zeros((c, c), F32)
        m = c // 2
        while m >= HG_DIAG:
            piv = jnp.concatenate(
                [jnp.broadcast_to(b[blk * 2 * m + m - 1: blk * 2 * m + m, :], (2 * m, HG_DIM))
                 for blk in range(c // (2 * m))], axis=0)
            x = jnp.exp(-jnp.abs(b - piv))
            upper = (row & m) != 0
            qm = (jnp.where(upper, q, 0.0) * x).astype(BF16)
            km = (jnp.where(upper, 0.0, kc) * x).astype(BF16)
            am = lax.dot_general(qm, km, NT_DIMS, preferred_element_type=F32)
            same = (row // (2 * m)) == (col // (2 * m))
            a = a + jnp.where(same, am, 0.0)
            m //= 2

        gd = None
        for d in range(HG_DIAG):
            if d == 0:
                xd = q * kc
            else:
                fd = f if d == 1 else pltpu.roll(f, d - 1, axis=0)
                gd = fd if gd is None else gd * fd
                xd = q * pltpu.roll(kc, d, axis=0) * gd
            rd = jnp.sum(xd, axis=-1, keepdims=True)
            rd = jnp.where((row & (HG_DIAG - 1)) >= d, rd, 0.0)
            a = a + jnp.where(col == row - d, rd, 0.0)

        st = st_ref[...]
        qd = (q * jnp.exp(b)).astype(BF16)
        o = lax.dot_general(qd, st.astype(BF16), NT_DIMS, preferred_element_type=F32)
        o = o + jnp.dot(a.astype(BF16), v, preferred_element_type=F32)

        b_last = b[c - 1:c, :]
        kd = (kc * jnp.exp(b_last - b)).astype(BF16)
        st_ref[...] = st * jnp.exp(b_last) + lax.dot_general(v, kd, TN_DIMS, preferred_element_type=F32)

        y = _rms_rows(o, nw_ref[...]) * (gz * jax.nn.sigmoid(gz))
        o_ref[rows, :] = y.astype(o_ref.dtype)
        return carry

    lax.fori_loop(0, q_ref.shape[0] // c, chunk, 0)


def _hgrn(z3, lb, nw):
    bsz, tp, _ = z3.shape

    def zspec(group):
        return pl.BlockSpec((None, HG_STEP, HG_DIM), lambda b, h, s: (b, s, group * HG_HEADS + h))

    return pl.pallas_call(
        _hgrn_kernel,
        grid=(bsz, HG_HEADS, tp // HG_STEP),
        in_specs=[zspec(0), zspec(1), zspec(2), zspec(3),
                  pl.BlockSpec((1, HG_DIM), lambda b, h, s: (0, h)),
                  pl.BlockSpec((1, HG_DIM), lambda b, h, s: (0, 0))],
        out_specs=pl.BlockSpec((None, HG_STEP, HG_DIM), lambda b, h, s: (b, s, h)),
        out_shape=jax.ShapeDtypeStruct((bsz, tp, HG_HEADS * HG_DIM), BF16),
        scratch_shapes=[pltpu.VMEM((HG_DIM, HG_DIM), F32)],
        compiler_params=_cparams(("parallel", "parallel", "arbitrary")),
        name="hgrn2",
    )(z3, z3, z3, z3, lb, nw)


def _dsa_index_kernel(iq_ref, wt_ref, kt_ref, kb_ref, bias_ref, keys_ref, hi_ref, lo_ref, acc_ref, *, topk):
    tq = iq_ref.shape[0]
    kc = DSA_KC
    tp = kt_ref.shape[0]
    q0 = pl.program_id(1) * tq
    nchunks = (q0 + tq - 1) // kc + 1
    qpos = q0 + lax.broadcasted_iota(jnp.int32, (1, tq), 1)
    int_min = jnp.int32(INT_MIN)
    half = jnp.int32(HALF_OFF)

    def score_chunk(ci, carry):
        k0 = pl.multiple_of(ci * kc, kc)
        kt = kt_ref[pl.ds(k0, kc), :]
        kb = kb_ref[pl.ds(k0, kc), :]
        for p in range(IDX_HEADS // 2):
            qp = iq_ref[:, p * LANES:(p + 1) * LANES]
            st = lax.dot_general(kt, qp, NT_DIMS, preferred_element_type=F32)
            sb = lax.dot_general(kb, qp, NT_DIMS, preferred_element_type=F32)
            part = (wt_ref[2 * p:2 * p + 1, :] * jnp.maximum(st, 0.0)
                    + wt_ref[2 * p + 1:2 * p + 2, :] * jnp.maximum(sb, 0.0))
            if p == 0:
                acc_ref[...] = part
            else:
                acc_ref[...] += part
        score = acc_ref[...] + 0.0
        bits = lax.bitcast_convert_type(score, jnp.int32)
        key = bits ^ ((bits >> 31) & jnp.int32(0x7FFFFFFF))
        kpos = k0 + lax.broadcasted_iota(jnp.int32, (kc, 1), 0)
        valid = (kpos >= META0) & (kpos <= qpos)
        key = jnp.where(valid, key, int_min)
        keys_ref[pl.ds(k0, kc), :] = key
        hi_ref[pl.ds(k0, kc), :] = (key >> 16).astype(jnp.int16)
        lo_ref[pl.ds(k0, kc), :] = (((key << 16) ^ int_min) >> 16).astype(jnp.int16)
        return carry

    lax.fori_loop(0, nchunks, score_chunk, 0)

    def count_ge(src_ref, cand_u):
        cand = (cand_u - half).astype(jnp.int16)

        def count_chunk(ci, cnt):
            k0 = pl.multiple_of(ci * kc, kc)
            for r in range(kc // COUNT_ROWS):
                ks = src_ref[pl.ds(k0 + r * COUNT_ROWS, COUNT_ROWS), :]
                cnt = cnt + jnp.where(ks >= cand, jnp.int16(1), jnp.int16(0))
            return cnt

        cnt = lax.fori_loop(0, nchunks, count_chunk, jnp.zeros((COUNT_ROWS, tq), jnp.int16))
        return jnp.sum(cnt.astype(F32), axis=0, keepdims=True)

    def bisect16(src_ref, need):
        def bit_pass(i, u):
            cand_u = u | lax.shift_left(jnp.int32(1), 15 - i)
            return jnp.where(count_ge(src_ref, cand_u) >= need, cand_u, u)
        return lax.fori_loop(0, 16, bit_pass, jnp.zeros((1, tq), jnp.int32))

    u_hi = bisect16(hi_ref, jnp.full((1, tq), float(topk), F32))
    top = 2 * HALF_OFF - 1
    above = jnp.where(u_hi == top, 0.0, count_ge(hi_ref, jnp.minimum(u_hi + 1, top)))
    hi_thr = (u_hi - half).astype(jnp.int16)

    def narrow_chunk(ci, carry):
        rows = pl.ds(pl.multiple_of(ci * kc, kc), kc)
        lo_ref[rows, :] = jnp.where(hi_ref[rows, :] == hi_thr, lo_ref[rows, :], jnp.int16(-HALF_OFF))
        return carry

    lax.fori_loop(0, nchunks, narrow_chunk, 0)
    u_lo = bisect16(lo_ref, topk - above)
    thr = jnp.where(u_hi == 0, int_min + 1, ((u_hi - half) << 16) | u_lo)

    def write_chunk(ci, carry):
        k0 = pl.multiple_of(ci * kc, kc)
        ks = keys_ref[pl.ds(k0, kc), :]
        bias_ref[pl.ds(k0, kc), :] = jnp.where(ks >= thr, 0.0, NEG_BIG).astype(bias_ref.dtype)
        return carry

    lax.fori_loop(0, nchunks, write_chunk, 0)

    def fill_chunk(ci, carry):
        k0 = pl.multiple_of(ci * kc, kc)
        bias_ref[pl.ds(k0, kc), :] = jnp.full((kc, tq), NEG_BIG, bias_ref.dtype)
        return carry

    lax.fori_loop(nchunks, tp // kc, fill_chunk, 0)


def _dsa_index(z3, wt, kt, kb, topk):
    bsz, tp, _ = z3.shape
    iq_col = (HG_HEADS * HG_DIM * 4 + AT_HEADS * AT_HDIM * 3) // (IDX_HEADS * IDX_HDIM)
    return pl.pallas_call(
        functools.partial(_dsa_index_kernel, topk=topk),
        grid=(bsz, tp // DSA_TQ),
        in_specs=[pl.BlockSpec((None, DSA_TQ, IDX_HEADS * IDX_HDIM), lambda b, i: (b, i, iq_col)),
                  pl.BlockSpec((None, IDX_HEADS, DSA_TQ), lambda b, i: (b, 0, i)),
                  pl.BlockSpec((None, tp, LANES), lambda b, i: (b, 0, 0)),
                  pl.BlockSpec((None, tp, LANES), lambda b, i: (b, 0, 0))],
        out_specs=pl.BlockSpec((None, tp, DSA_TQ), lambda b, i: (b, 0, i)),
        out_shape=jax.ShapeDtypeStruct((bsz, tp, tp), BF16),
        scratch_shapes=[pltpu.VMEM((tp, DSA_TQ), jnp.int32), pltpu.VMEM((tp, DSA_TQ), jnp.int16),
                        pltpu.VMEM((tp, DSA_TQ), jnp.int16), pltpu.VMEM((DSA_KC, DSA_TQ), F32)],
        compiler_params=_cparams(("parallel", "arbitrary")),
        name="dsa_index",
    )(z3, wt, kt, kb)


def _dsa_attn_kernel(aq_ref, k_ref, vt_ref, bias_ref, o_ref, qs_ref, acc_ref, lg_ref, p_ref):
    tq = aq_ref.shape[0]
    kc = DSA_KC
    heads = aq_ref.shape[1] // AT_HDIM
    q0 = pl.program_id(2) * tq
    nchunks = (q0 + tq - 1) // kc + 1

    qs_ref[...] = (aq_ref[...].astype(F32) * (AT_HDIM ** -0.5 * math.log2(math.e))).astype(qs_ref.dtype)
    acc_ref[...] = jnp.zeros(acc_ref.shape, F32)

    def chunk(ci, carry):
        ms, ls = carry
        k0 = pl.multiple_of(ci * kc, kc)
        bias = bias_ref[pl.ds(k0, kc), :].astype(F32)
        new_ms, new_ls, alphas = [], [], []
        for h in range(heads):
            hs = slice(h * AT_HDIM, (h + 1) * AT_HDIM)
            lg = lax.dot_general(k_ref[pl.ds(k0, kc), hs], qs_ref[:, hs], NT_DIMS,
                                 preferred_element_type=F32) + bias
            lg_ref[h] = lg
            new_ms.append(jnp.maximum(ms[h], _rows_reduce(lg, jnp.maximum, jnp.max)))
        for h in range(heads):
            alphas.append(jnp.exp2(ms[h] - new_ms[h]))
            p = jnp.exp2(lg_ref[h] - new_ms[h])
            p_ref[h] = p.astype(BF16)
            new_ls.append(alphas[h] * ls[h] + _rows_reduce(p, jnp.add, jnp.sum))
        for h in range(heads):
            hs = slice(h * AT_HDIM, (h + 1) * AT_HDIM)
            pv = jnp.dot(vt_ref[ci, hs, :], p_ref[h], preferred_element_type=F32)
            acc_ref[h] = alphas[h] * acc_ref[h] + pv
        return tuple(new_ms), tuple(new_ls)

    init = (tuple(jnp.full((1, tq), NEG_BIG, F32) for _ in range(heads)),
            tuple(jnp.zeros((1, tq), F32) for _ in range(heads)))
    _, ls = lax.fori_loop(0, nchunks, chunk, init)

    live = (q0 + lax.broadcasted_iota(jnp.int32, (1, tq), 1)) >= META0
    for h in range(heads):
        out_t = jnp.where(live, acc_ref[h] / ls[h], 0.0)
        o_ref[:, h * AT_HDIM:(h + 1) * AT_HDIM] = out_t.T.astype(o_ref.dtype)


def _dsa_attn(z3, vt, bias):
    bsz, tp, _ = z3.shape
    gw = AT_HEADS * AT_HDIM // ATT_GROUPS
    aq_col = HG_HEADS * HG_DIM * 4 // gw
    ak_col = aq_col + ATT_GROUPS
    hpg = AT_HEADS // ATT_GROUPS
    return pl.pallas_call(
        _dsa_attn_kernel,
        grid=(bsz, ATT_GROUPS, tp // DSA_TQ),
        in_specs=[pl.BlockSpec((None, DSA_TQ, gw), lambda b, g, i: (b, i, aq_col + g)),
                  pl.BlockSpec((None, tp, gw), lambda b, g, i: (b, 0, ak_col + g)),
                  pl.BlockSpec((None, tp // DSA_KC, gw, DSA_KC), lambda b, g, i: (b, 0, g, 0)),
                  pl.BlockSpec((None, tp, DSA_TQ), lambda b, g, i: (b, 0, i))],
        out_specs=pl.BlockSpec((None, DSA_TQ, gw), lambda b, g, i: (b, i, g)),
        out_shape=jax.ShapeDtypeStruct((bsz, tp, AT_HEADS * AT_HDIM), BF16),
        scratch_shapes=[pltpu.VMEM((DSA_TQ, gw), BF16),
                        pltpu.VMEM((hpg, AT_HDIM, DSA_TQ), F32),
                        pltpu.VMEM((hpg, DSA_KC, DSA_TQ), F32),
                        pltpu.VMEM((hpg, DSA_KC, DSA_TQ), BF16)],
        compiler_params=_cparams(("parallel", "parallel", "arbitrary")),
        name="dsa_attn",
    )(z3, z3, vt, bias)


def _merge_kernel(yh_ref, ya_ref, ga_ref, gb_ref, wh_ref, wa_ref, o_ref):
    mh = jnp.dot(yh_ref[...], wh_ref[...], preferred_element_type=F32)
    ma = jnp.dot(ya_ref[...], wa_ref[...], preferred_element_type=F32)
    o = jax.nn.sigmoid(ga_ref[...].astype(F32)) * mh + jax.nn.sigmoid(gb_ref[...].astype(F32)) * ma
    o_ref[...] = o.astype(o_ref.dtype)


def _merge(yh, ya, z, wh, wa, tm, tn):
    m, kdim = yh.shape
    n = wh.shape[1]
    ga_col = (z.shape[1] - 2 * n) // tn
    gb_col = (z.shape[1] - n) // tn
    return pl.pallas_call(
        _merge_kernel,
        grid=(m // tm, n // tn),
        in_specs=[pl.BlockSpec((tm, kdim), lambda i, j: (i, 0)),
                  pl.BlockSpec((tm, kdim), lambda i, j: (i, 0)),
                  pl.BlockSpec((tm, tn), lambda i, j: (i, ga_col + j)),
                  pl.BlockSpec((tm, tn), lambda i, j: (i, gb_col + j)),
                  pl.BlockSpec((kdim, tn), lambda i, j: (0, j)),
                  pl.BlockSpec((kdim, tn), lambda i, j: (0, j))],
        out_specs=pl.BlockSpec((tm, tn), lambda i, j: (i, j)),
        out_shape=jax.ShapeDtypeStruct((m, n), BF16),
        compiler_params=_cparams(("parallel", "arbitrary")),
        name="branch_merge",
    )(yh, ya, z, z, wh, wa)


def _out_proj_kernel(x_ref, pre_ref, m_ref, w_ref, nw_ref, h1_ref, u2_ref, *, bps):
    is_prefix = pl.program_id(0) % bps == 0
    h = jnp.where(is_prefix, pre_ref[...], x_ref[...])
    h1 = h + jnp.dot(m_ref[...], w_ref[...], preferred_element_type=F32)
    h1_ref[...] = h1
    u2_ref[...] = _rms_rows(h1, nw_ref[...]).astype(u2_ref.dtype)


def _out_proj(x2, prefix, mm, w, nw, bsz, tp):
    m, d = mm.shape
    tm = SEQ_OFF
    bps = tp // tm
    row = lambda r: (r, 0)
    fixed = lambda r: (0, 0)
    return pl.pallas_call(
        functools.partial(_out_proj_kernel, bps=bps),
        grid=(m // tm,),
        in_specs=[pl.BlockSpec((tm, d), lambda r: (_x_block(r, bps), 0)), pl.BlockSpec((tm, d), fixed),
                  pl.BlockSpec((tm, d), row), pl.BlockSpec((d, d), fixed), pl.BlockSpec((1, d), fixed)],
        out_specs=[pl.BlockSpec((tm, d), row), pl.BlockSpec((tm, d), row)],
        out_shape=[jax.ShapeDtypeStruct((m, d), F32), jax.ShapeDtypeStruct((m, d), BF16)],
        compiler_params=_cparams(("parallel",)),
        name="out_proj",
    )(x2, prefix, mm, w, nw)


def _ffn_up_kernel(u_ref, wg_ref, wu_ref, cw_ref, cb_ref, o_ref, tail_ref):
    tm = u_ref.shape[0]

    @pl.when(pl.program_id(1) == 0)
    def _():
        tail_ref[...] = jnp.zeros_like(tail_ref)

    u = u_ref[...]
    g0 = jnp.dot(u, wg_ref[...], preferred_element_type=F32)
    up = jnp.dot(u, wu_ref[...], preferred_element_type=F32)
    row = lax.broadcasted_iota(jnp.int32, (tm, 1), 0)
    prev1 = tail_ref[7:8, :]
    prev2 = tail_ref[6:7, :]
    g1 = jnp.where(row == 0, prev1, pltpu.roll(g0, 1, axis=0))
    g2 = jnp.where(row == 0, prev2, jnp.where(row == 1, prev1, pltpu.roll(g0, 2, axis=0)))
    tail_ref[...] = g0[tm - 8:tm, :]
    a = cw_ref[0:1, :] * g2 + cw_ref[1:2, :] * g1 + cw_ref[2:3, :] * g0 + cb_ref[...]
    o_ref[...] = (a * jax.nn.sigmoid(a) * up).astype(o_ref.dtype)


def _ffn_up(u2, wg, wu, cw, cb, tm, tn):
    m, d = u2.shape
    n = wg.shape[1]
    return pl.pallas_call(
        _ffn_up_kernel,
        grid=(n // tn, m // tm),
        in_specs=[pl.BlockSpec((tm, d), lambda j, i: (i, 0)),
                  pl.BlockSpec((d, tn), lambda j, i: (0, j)),
                  pl.BlockSpec((d, tn), lambda j, i: (0, j)),
                  pl.BlockSpec((8, tn), lambda j, i: (0, j)),
                  pl.BlockSpec((1, tn), lambda j, i: (0, j))],
        out_specs=pl.BlockSpec((tm, tn), lambda j, i: (i, j)),
        out_shape=jax.ShapeDtypeStruct((m, n), BF16),
        scratch_shapes=[pltpu.VMEM((8, tn), F32)],
        compiler_params=_cparams(("parallel", "arbitrary")),
        name="ffn_up",
    )(u2, wg, wu, cw, cb)


def _ffn_down_kernel(g_ref, wd_ref, h1_ref, nw_ref, o_ref):
    h2 = h1_ref[...] + jnp.dot(g_ref[...], wd_ref[...], preferred_element_type=F32)
    o_ref[...] = _rms_rows(h2, nw_ref[...])


def _ffn_down(gact, wd, h1, nw, bsz, seq, tp, tm):
    m, kdim = gact.shape
    d = wd.shape[1]
    per_seq = seq // tm
    off = SEQ_OFF // tm

    def row_in(i):
        return ((i // per_seq) * (tp // tm) + off + i % per_seq, 0)

    return pl.pallas_call(
        _ffn_down_kernel,
        grid=(bsz * per_seq,),
        in_specs=[pl.BlockSpec((tm, kdim), row_in),
                  pl.BlockSpec((kdim, d), lambda i: (0, 0), pipeline_mode=pl.Buffered(1)),
                  pl.BlockSpec((tm, d), row_in),
                  pl.BlockSpec((1, d), lambda i: (0, 0))],
        out_specs=pl.BlockSpec((tm, d), lambda i: (i, 0)),
        out_shape=jax.ShapeDtypeStruct((bsz * seq, d), F32),
        compiler_params=_cparams(("parallel",)),
        name="ffn_down",
    )(gact, wd, h1, nw)


def kernel(x, meta_tokens, attn_norm_w, w_in, hgrn_lb_logits, hgrn_norm_w, idx_k_norm_w, w_branch_hgrn,
           w_branch_dsa, w_out, ffn_norm_w, w_ffn_gate, w_ffn_up, ffn_conv_w, ffn_conv_b, w_ffn_down,
           final_norm_w):
    bsz, seq, d = x.shape
    depth = w_in.shape[0]
    topk = min(IDX_TOPK_MAX, seq // 4)
    tp = SEQ_OFF + seq
    m = bsz * tp
    hg_w = HG_HEADS * HG_DIM
    at_w = AT_HEADS * AT_HDIM
    n_main_lo = 4 * hg_w + 3 * at_w + IDX_HEADS * IDX_HDIM
    n_small = IDX_HDIM + IDX_HEADS

    prefix = jnp.concatenate([jnp.zeros((META0, d), x.dtype), meta_tokens.astype(x.dtype)], axis=0)
    x2 = x.reshape(bsz * seq, d)
    lbs = jnp.cumsum(jax.nn.softmax(hgrn_lb_logits.astype(F32), axis=0), axis=0)

    assert depth == 1, "multi-layer stacks are not supported"
    out = None
    for l in range(depth):
        w_l = w_in[l]
        w_lo = w_l[:, :n_main_lo].astype(BF16)
        w_hi = w_l[:, n_main_lo + n_small:].astype(BF16)
        w_small = jnp.pad(w_l[:, n_main_lo:n_main_lo + n_small], ((0, 0), (0, LANES - n_small))).astype(BF16)
        knw = jnp.pad(idx_k_norm_w[l], (0, LANES - IDX_HDIM)).reshape(1, LANES)

        u, zs, kt, kb = _embed(x2, prefix, attn_norm_w[l].reshape(1, d), w_small, knw, bsz, tp)
        z = _in_proj(u, w_lo, w_hi, tm=1024, tn=1024)
        z3 = z.reshape(bsz, tp, z.shape[1])

        y_h = _hgrn(z3, lbs[l].reshape(1, hg_w), hgrn_norm_w[l].reshape(1, HG_DIM))

        wt = zs.reshape(bsz, tp, LANES)[:, :, IDX_HDIM:IDX_HDIM + IDX_HEADS]
        wt = jnp.swapaxes(wt * (IDX_HEADS ** -0.5 * IDX_HDIM ** -0.5), 1, 2)
        bias = _dsa_index(z3, wt, kt.reshape(bsz, tp, LANES), kb.reshape(bsz, tp, LANES), topk)
        av = z3[:, :, 4 * hg_w + 2 * at_w:4 * hg_w + 3 * at_w]
        vt = jnp.swapaxes(av.reshape(bsz, tp // DSA_KC, DSA_KC, at_w), 2, 3)
        y_a = _dsa_attn(z3, vt, bias)

        mm = _merge(y_h.reshape(m, hg_w), y_a.reshape(m, at_w), z,
                    w_branch_hgrn[l].astype(BF16), w_branch_dsa[l].astype(BF16), tm=1024, tn=1024)
        h1, u2 = _out_proj(x2, prefix, mm, w_out[l].astype(BF16), ffn_norm_w[l].reshape(1, d), bsz, tp)

        cw = jnp.pad(ffn_conv_w[l], ((0, 8 - CONV_W), (0, 0)))
        gact = _ffn_up(u2, w_ffn_gate[l].astype(BF16), w_ffn_up[l].astype(BF16), cw,
                       ffn_conv_b[l].reshape(1, -1), tm=1024, tn=512)
        out = _ffn_down(gact, w_ffn_down[l].astype(BF16), h1, final_norm_w.reshape(1, d),
                        bsz, seq, tp, tm=256)
    return out.reshape(bsz, seq, d)
```

```python
import functools
import math

import jax
import jax.numpy as jnp
import numpy as np
from jax import lax
from jax.experimental import pallas as pl
from jax.experimental.pallas import tpu as pltpu

N_META = 16
EPS = 1e-6
HG_HEADS = 8
HG_DIM = 128
AT_HEADS = 8
AT_HDIM = 128
IDX_HEADS = 16
IDX_HDIM = 64
IDX_TOPK_MAX = 256
CONV_W = 3

LANES = 128
SUBLANES = 8
SEQ_OFF = 512
META0 = SEQ_OFF - N_META
NEG_BIG = -1e30
INT_MIN = -(2 ** 31)

HG_CHUNK = 128
HG_DIAG = 4
SCAN_STEPS = (1, 2, 4)
HG_STEP = 512
DSA_TQ = 256
DSA_KC = 512
COUNT_ROWS = 64
HALF_OFF = 2 ** 15
ATT_GROUPS = 1
VMEM_LIMIT = 56 * 1024 * 1024

F32 = jnp.float32
BF16 = jnp.bfloat16
NT_DIMS = (((1,), (1,)), ((), ()))
TN_DIMS = (((0,), (0,)), ((), ()))


def _cparams(sem):
    return pltpu.CompilerParams(dimension_semantics=sem, vmem_limit_bytes=VMEM_LIMIT)


def _rms_rows(x, w):
    ms = jnp.mean(x * x, axis=-1, keepdims=True)
    return x * lax.rsqrt(ms + EPS) * w


def _rows_reduce(x, pair_op, final_reduce):
    rows = x.shape[0]
    while rows > 64:
        rows //= 2
        x = pair_op(x[:rows], x[rows:])
    return final_reduce(x, axis=0, keepdims=True)


def _x_block(r, bps):
    return (r // bps) * (bps - 1) + jnp.maximum(r % bps - 1, 0)


def _embed_kernel(x_ref, pre_ref, nw_ref, w_ref, knw_ref, u_ref, zs_ref, kt_ref, kb_ref, *, bps):
    is_prefix = pl.program_id(0) % bps == 0
    h = jnp.where(is_prefix, pre_ref[...], x_ref[...])
    u = _rms_rows(h, nw_ref[...]).astype(BF16)
    u_ref[...] = u
    zs = jnp.dot(u, w_ref[...], preferred_element_type=F32)
    lane = lax.broadcasted_iota(jnp.int32, zs.shape, 1)
    ik = jnp.where(lane < IDX_HDIM, zs, 0.0)
    ms = jnp.sum(ik * ik, axis=-1, keepdims=True) / IDX_HDIM
    kn = ik * lax.rsqrt(ms + EPS) * knw_ref[...]
    zs_ref[...] = zs
    kt_ref[...] = kn.astype(BF16)
    kb_ref[...] = pltpu.roll(kn, IDX_HDIM, axis=1).astype(BF16)


def _embed(x2, prefix, nw, w, knw, bsz, tp):
    d = x2.shape[1]
    tm = SEQ_OFF
    bps = tp // tm
    m = bsz * tp
    row = lambda r: (r, 0)
    fixed = lambda r: (0, 0)
    return pl.pallas_call(
        functools.partial(_embed_kernel, bps=bps),
        grid=(m // tm,),
        in_specs=[pl.BlockSpec((tm, d), lambda r: (_x_block(r, bps), 0)),
                  pl.BlockSpec((tm, d), fixed), pl.BlockSpec((1, d), fixed),
                  pl.BlockSpec((d, LANES), fixed), pl.BlockSpec((1, LANES), fixed)],
        out_specs=[pl.BlockSpec((tm, d), row)] + [pl.BlockSpec((tm, LANES), row)] * 3,
        out_shape=[jax.ShapeDtypeStruct((m, d), BF16),
                   jax.ShapeDtypeStruct((m, LANES), F32),
                   jax.ShapeDtypeStruct((m, LANES), BF16),
                   jax.ShapeDtypeStruct((m, LANES), BF16)],
        compiler_params=_cparams(("parallel",)),
        name="embed_norm",
    )(x2, prefix, nw, w, knw)


def _in_proj_kernel(u_ref, wt_ref, o_ref, w_scr):
    @pl.when(pl.program_id(1) == 0)
    def _():
        w_scr[...] = wt_ref[...].astype(BF16)

    o_ref[...] = lax.dot_general(u_ref[...], w_scr[...], NT_DIMS,
                                 preferred_element_type=F32).astype(o_ref.dtype)


def _in_proj(u, wt, n_lo_cols, hi_start, n_hi_cols, tm, tn):
    m, d = u.shape
    n_lo = n_lo_cols // tn
    n_blocks = n_lo + n_hi_cols // tn

    assert hi_start % SUBLANES == 0

    def w_rows(j, i):
        return (pl.multiple_of(jnp.where(j < n_lo, j * tn, hi_start + (j - n_lo) * tn), SUBLANES), 0)

    return pl.pallas_call(
        _in_proj_kernel,
        grid=(n_blocks, m // tm),
        in_specs=[pl.BlockSpec((tm, d), lambda j, i: (i, 0)),
                  pl.BlockSpec((pl.Element(tn), pl.Element(d)), w_rows)],
        out_specs=pl.BlockSpec((tm, tn), lambda j, i: (i, j)),
        out_shape=jax.ShapeDtypeStruct((m, n_blocks * tn), BF16),
        scratch_shapes=[pltpu.VMEM((tn, d), BF16)],
        compiler_params=_cparams(("parallel", "arbitrary")),
        name="in_proj",
    )(u, wt)


def _hgrn_masks():
    c = HG_CHUNK
    t = np.arange(c)[:, None]
    s = np.arange(c)[None, :]
    scan = [np.broadcast_to((t % 8) >= k, (c, c)) for k in SCAN_STEPS]
    levels = [(t // (2 * m) == s // (2 * m)) & ((t // m) % 2 == 1) & ((s // m) % 2 == 0) for m in _hgrn_levels()]
    diags = [(s == t - d) & ((t % HG_DIAG) >= d) for d in range(HG_DIAG)]
    return np.stack(scan + levels + diags).astype(np.float32)


def _hgrn_levels():
    m, out = HG_CHUNK // 2, []
    while m >= HG_DIAG:
        out.append(m)
        m //= 2
    return out


def _sigmoid(x):
    return 1.0 / (1.0 + jnp.exp(-x))


def _hgrn_kernel(q_ref, f_ref, i_ref, g_ref, lb_ref, nw_ref, mk_ref, o_ref, st_ref):
    c = HG_CHUNK
    levels = _hgrn_levels()
    n_scan = len(SCAN_STEPS)

    @pl.when(pl.program_id(2) == 0)
    def _():
        st_ref[...] = jnp.zeros_like(st_ref)

    lb = lb_ref[...]
    oml = 1.0 - lb

    def chunk(ci, st):
        rows = pl.ds(ci * c, c)
        qz = q_ref[rows, :].astype(F32)
        fz = f_ref[rows, :].astype(F32)
        v = i_ref[rows, :]
        gz = g_ref[rows, :].astype(F32)

        q = qz * _sigmoid(qz)
        sf = _sigmoid(fz)
        f = lb + oml * sf
        kc = oml * (1.0 - sf)

        bw = jnp.log2(f)
        for k, step in enumerate(SCAN_STEPS):
            bw = bw + pltpu.roll(bw, step, axis=0) * mk_ref[k]
        run = jnp.zeros((1, HG_DIM), F32)
        tiles = []
        for i in range(c // 8):
            tiles.append(bw[8 * i:8 * i + 8, :] + run)
            run = run + bw[8 * i + 7:8 * i + 8, :]
        b = jnp.concatenate(tiles, axis=0)

        a = jnp.zeros((c, c), F32)
        for k, m in enumerate(levels):
            piv = jnp.concatenate(
                [jnp.broadcast_to(b[blk * 2 * m + m - 1: blk * 2 * m + m, :], (2 * m, HG_DIM))
                 for blk in range(c // (2 * m))], axis=0)
            x = jnp.exp2(-jnp.abs(b - piv))
            am = lax.dot_general((q * x).astype(BF16), (kc * x).astype(BF16), NT_DIMS,
                                 preferred_element_type=F32)
            a = a + am * mk_ref[n_scan + k]

        gd = None
        for d in range(HG_DIAG):
            if d == 0:
                xd = q * kc
            else:
                fd = f if d == 1 else pltpu.roll(f, d - 1, axis=0)
                gd = fd if gd is None else gd * fd
                xd = q * pltpu.roll(kc, d, axis=0) * gd
            a = a + jnp.sum(xd, axis=-1, keepdims=True) * mk_ref[n_scan + len(levels) + d]

        qd = (q * jnp.exp2(b)).astype(BF16)
        o = lax.dot_general(qd, st.astype(BF16), NT_DIMS, preferred_element_type=F32)
        o = o + jnp.dot(a.astype(BF16), v, preferred_element_type=F32)

        b_last = b[c - 1:c, :]
        kd = (kc * jnp.exp2(b_last - b)).astype(BF16)
        st = st * jnp.exp2(b_last) + lax.dot_general(v, kd, TN_DIMS, preferred_element_type=F32)

        y = _rms_rows(o, nw_ref[...]) * (gz * _sigmoid(gz))
        o_ref[rows, :] = y.astype(o_ref.dtype)
        return st

    st = st_ref[...]
    for ci in range(q_ref.shape[0] // c):
        st = chunk(ci, st)
    st_ref[...] = st


def _hgrn(z3, lb, nw):
    bsz, tp, _ = z3.shape
    masks = jnp.asarray(_hgrn_masks())

    def zspec(group):
        return pl.BlockSpec((None, HG_STEP, HG_DIM), lambda b, h, s: (b, s, group * HG_HEADS + h))

    return pl.pallas_call(
        _hgrn_kernel,
        grid=(bsz, HG_HEADS, tp // HG_STEP),
        in_specs=[zspec(0), zspec(1), zspec(2), zspec(3),
                  pl.BlockSpec((1, HG_DIM), lambda b, h, s: (0, h)),
                  pl.BlockSpec((1, HG_DIM), lambda b, h, s: (0, 0)),
                  pl.BlockSpec(masks.shape, lambda b, h, s: (0, 0, 0))],
        out_specs=pl.BlockSpec((None, HG_STEP, HG_DIM), lambda b, h, s: (b, s, h)),
        out_shape=jax.ShapeDtypeStruct((bsz, tp, HG_HEADS * HG_DIM), BF16),
        scratch_shapes=[pltpu.VMEM((HG_DIM, HG_DIM), F32)],
        compiler_params=_cparams(("parallel", "parallel", "arbitrary")),
        name="hgrn2",
    )(z3, z3, z3, z3, lb, nw, masks)


def _dsa_index_kernel(iq_ref, wt_ref, kt_ref, kb_ref, bias_ref, keys_ref, hi_ref, lo_ref, acc_ref, *, topk):
    tq = iq_ref.shape[0]
    kc = DSA_KC
    tp = kt_ref.shape[0]
    q0 = pl.program_id(1) * tq
    nchunks = (q0 + tq - 1) // kc + 1
    qpos = q0 + lax.broadcasted_iota(jnp.int32, (1, tq), 1)
    int_min = jnp.int32(INT_MIN)
    half = jnp.int32(HALF_OFF)

    def score_chunk(ci, carry):
        k0 = pl.multiple_of(ci * kc, kc)
        kt = kt_ref[pl.ds(k0, kc), :]
        kb = kb_ref[pl.ds(k0, kc), :]
        for p in range(IDX_HEADS // 2):
            qp = iq_ref[:, p * LANES:(p + 1) * LANES]
            st = lax.dot_general(kt, qp, NT_DIMS, preferred_element_type=F32)
            sb = lax.dot_general(kb, qp, NT_DIMS, preferred_element_type=F32)
            part = (wt_ref[2 * p:2 * p + 1, :] * jnp.maximum(st, 0.0)
                    + wt_ref[2 * p + 1:2 * p + 2, :] * jnp.maximum(sb, 0.0))
            if p == 0:
                acc_ref[...] = part
            else:
                acc_ref[...] += part
        score = acc_ref[...] + 0.0
        bits = lax.bitcast_convert_type(score, jnp.int32)
        key = bits ^ ((bits >> 31) & jnp.int32(0x7FFFFFFF))
        kpos = k0 + lax.broadcasted_iota(jnp.int32, (kc, 1), 0)
        valid = (kpos >= META0) & (kpos <= qpos)
        key = jnp.where(valid, key, int_min)
        keys_ref[pl.ds(k0, kc), :] = key
        hi_ref[pl.ds(k0, kc), :] = (key >> 16).astype(jnp.int16)
        lo_ref[pl.ds(k0, kc), :] = (((key << 16) ^ int_min) >> 16).astype(jnp.int16)
        return carry

    lax.fori_loop(0, nchunks, score_chunk, 0)

    def count_ge(src_ref, cand_u):
        cand = (cand_u - half).astype(jnp.int16)

        def count_chunk(ci, cnt):
            k0 = pl.multiple_of(ci * kc, kc)
            for r in range(kc // COUNT_ROWS):
                ks = src_ref[pl.ds(k0 + r * COUNT_ROWS, COUNT_ROWS), :]
                cnt = cnt + jnp.where(ks >= cand, jnp.int16(1), jnp.int16(0))
            return cnt

        cnt = lax.fori_loop(0, nchunks, count_chunk, jnp.zeros((COUNT_ROWS, tq), jnp.int16))
        return jnp.sum(cnt.astype(F32), axis=0, keepdims=True)

    def bisect16(src_ref, need):
        def bit_pass(i, u):
            cand_u = u | lax.shift_left(jnp.int32(1), 15 - i)
            return jnp.where(count_ge(src_ref, cand_u) >= need, cand_u, u)
        return lax.fori_loop(0, 16, bit_pass, jnp.zeros((1, tq), jnp.int32))

    u_hi = bisect16(hi_ref, jnp.full((1, tq), float(topk), F32))
    top = 2 * HALF_OFF - 1
    above = jnp.where(u_hi == top, 0.0, count_ge(hi_ref, jnp.minimum(u_hi + 1, top)))
    hi_thr = (u_hi - half).astype(jnp.int16)

    def narrow_chunk(ci, carry):
        rows = pl.ds(pl.multiple_of(ci * kc, kc), kc)
        lo_ref[rows, :] = jnp.where(hi_ref[rows, :] == hi_thr, lo_ref[rows, :], jnp.int16(-HALF_OFF))
        return carry

    lax.fori_loop(0, nchunks, narrow_chunk, 0)
    u_lo = bisect16(lo_ref, topk - above)
    thr = jnp.where(u_hi == 0, int_min + 1, ((u_hi - half) << 16) | u_lo)

    def write_chunk(ci, carry):
        k0 = pl.multiple_of(ci * kc, kc)
        ks = keys_ref[pl.ds(k0, kc), :]
        bias_ref[pl.ds(k0, kc), :] = jnp.where(ks >= thr, 0.0, NEG_BIG).astype(bias_ref.dtype)
        return carry

    lax.fori_loop(0, nchunks, write_chunk, 0)

    def fill_chunk(ci, carry):
        k0 = pl.multiple_of(ci * kc, kc)
        bias_ref[pl.ds(k0, kc), :] = jnp.full((kc, tq), NEG_BIG, bias_ref.dtype)
        return carry

    lax.fori_loop(nchunks, tp // kc, fill_chunk, 0)


def _dsa_index(z3, wt, kt, kb, topk):
    bsz, tp, _ = z3.shape
    iq_col = (HG_HEADS * HG_DIM * 4 + AT_HEADS * AT_HDIM * 3) // (IDX_HEADS * IDX_HDIM)
    return pl.pallas_call(
        functools.partial(_dsa_index_kernel, topk=topk),
        grid=(bsz, tp // DSA_TQ),
        in_specs=[pl.BlockSpec((None, DSA_TQ, IDX_HEADS * IDX_HDIM), lambda b, i: (b, i, iq_col)),
                  pl.BlockSpec((None, IDX_HEADS, DSA_TQ), lambda b, i: (b, 0, i)),
                  pl.BlockSpec((None, tp, LANES), lambda b, i: (b, 0, 0)),
                  pl.BlockSpec((None, tp, LANES), lambda b, i: (b, 0, 0))],
        out_specs=pl.BlockSpec((None, None, tp, DSA_TQ), lambda b, i: (b, i, 0, 0)),
        out_shape=jax.ShapeDtypeStruct((bsz, tp // DSA_TQ, tp, DSA_TQ), BF16),
        scratch_shapes=[pltpu.VMEM((tp, DSA_TQ), jnp.int32), pltpu.VMEM((tp, DSA_TQ), jnp.int16),
                        pltpu.VMEM((tp, DSA_TQ), jnp.int16), pltpu.VMEM((DSA_KC, DSA_TQ), F32)],
        compiler_params=_cparams(("parallel", "arbitrary")),
        name="dsa_index",
    )(z3, wt, kt, kb)


def _dsa_attn_kernel(aq_ref, k_ref, vt_ref, bias_ref, o_ref, qs_ref, acc_ref, lg_ref, p_ref):
    tq = aq_ref.shape[0]
    kc = DSA_KC
    heads = aq_ref.shape[1] // AT_HDIM
    q0 = pl.program_id(2) * tq
    nchunks = (q0 + tq - 1) // kc + 1

    qs_ref[...] = (aq_ref[...].astype(F32) * (AT_HDIM ** -0.5 * math.log2(math.e))).astype(qs_ref.dtype)
    acc_ref[...] = jnp.zeros(acc_ref.shape, F32)

    def chunk(ci, carry):
        ms, ls = carry
        k0 = pl.multiple_of(ci * kc, kc)
        bias = bias_ref[pl.ds(k0, kc), :].astype(F32)
        new_ms, new_ls, alphas = [], [], []
        for h in range(heads):
            hs = slice(h * AT_HDIM, (h + 1) * AT_HDIM)
            lg = lax.dot_general(k_ref[pl.ds(k0, kc), hs], qs_ref[:, hs], NT_DIMS,
                                 preferred_element_type=F32) + bias
            lg_ref[h] = lg
            new_ms.append(jnp.maximum(ms[h], _rows_reduce(lg, jnp.maximum, jnp.max)))
        for h in range(heads):
            alphas.append(jnp.exp2(ms[h] - new_ms[h]))
            p = jnp.exp2(lg_ref[h] - new_ms[h])
            p_ref[h] = p.astype(BF16)
            new_ls.append(alphas[h] * ls[h] + _rows_reduce(p, jnp.add, jnp.sum))
        for h in range(heads):
            hs = slice(h * AT_HDIM, (h + 1) * AT_HDIM)
            pv = jnp.dot(vt_ref[ci, hs, :], p_ref[h], preferred_element_type=F32)
            acc_ref[h] = alphas[h] * acc_ref[h] + pv
        return tuple(new_ms), tuple(new_ls)

    init = (tuple(jnp.full((1, tq), NEG_BIG, F32) for _ in range(heads)),
            tuple(jnp.zeros((1, tq), F32) for _ in range(heads)))
    _, ls = lax.fori_loop(0, nchunks, chunk, init)

    live = (q0 + lax.broadcasted_iota(jnp.int32, (1, tq), 1)) >= META0
    for h in range(heads):
        out_t = jnp.where(live, acc_ref[h] / ls[h], 0.0)
        o_ref[:, h * AT_HDIM:(h + 1) * AT_HDIM] = out_t.T.astype(o_ref.dtype)


def _dsa_attn(z3, vt, bias):
    bsz, tp, _ = z3.shape
    gw = AT_HEADS * AT_HDIM // ATT_GROUPS
    aq_col = HG_HEADS * HG_DIM * 4 // gw
    ak_col = aq_col + ATT_GROUPS
    hpg = AT_HEADS // ATT_GROUPS
    return pl.pallas_call(
        _dsa_attn_kernel,
        grid=(bsz, ATT_GROUPS, tp // DSA_TQ),
        in_specs=[pl.BlockSpec((None, DSA_TQ, gw), lambda b, g, i: (b, i, aq_col + g)),
                  pl.BlockSpec((None, tp, gw), lambda b, g, i: (b, 0, ak_col + g), pipeline_mode=pl.Buffered(1)),
                  pl.BlockSpec((None, tp // DSA_KC, gw, DSA_KC), lambda b, g, i: (b, 0, g, 0),
                               pipeline_mode=pl.Buffered(1)),
                  pl.BlockSpec((None, None, tp, DSA_TQ), lambda b, g, i: (b, i, 0, 0))],
        out_specs=pl.BlockSpec((None, DSA_TQ, gw), lambda b, g, i: (b, i, g)),
        out_shape=jax.ShapeDtypeStruct((bsz, tp, AT_HEADS * AT_HDIM), BF16),
        scratch_shapes=[pltpu.VMEM((DSA_TQ, gw), BF16),
                        pltpu.VMEM((hpg, AT_HDIM, DSA_TQ), F32),
                        pltpu.VMEM((hpg, DSA_KC, DSA_TQ), F32),
                        pltpu.VMEM((hpg, DSA_KC, DSA_TQ), BF16)],
        compiler_params=_cparams(("parallel", "parallel", "arbitrary")),
        name="dsa_attn",
    )(z3, z3, vt, bias)


def _merge_kernel(yh_ref, ya_ref, ga_ref, gb_ref, wh_ref, wa_ref, o_ref):
    mh = jnp.dot(yh_ref[...], wh_ref[...], preferred_element_type=F32)
    ma = jnp.dot(ya_ref[...], wa_ref[...], preferred_element_type=F32)
    o = jax.nn.sigmoid(ga_ref[...].astype(F32)) * mh + jax.nn.sigmoid(gb_ref[...].astype(F32)) * ma
    o_ref[...] = o.astype(o_ref.dtype)


def _merge(yh, ya, z, wh, wa, tm, tn):
    m, kdim = yh.shape
    n = wh.shape[1]
    ga_col = (z.shape[1] - 2 * n) // tn
    gb_col = (z.shape[1] - n) // tn
    return pl.pallas_call(
        _merge_kernel,
        grid=(m // tm, n // tn),
        in_specs=[pl.BlockSpec((tm, kdim), lambda i, j: (i, 0)),
                  pl.BlockSpec((tm, kdim), lambda i, j: (i, 0)),
                  pl.BlockSpec((tm, tn), lambda i, j: (i, ga_col + j)),
                  pl.BlockSpec((tm, tn), lambda i, j: (i, gb_col + j)),
                  pl.BlockSpec((kdim, tn), lambda i, j: (0, j)),
                  pl.BlockSpec((kdim, tn), lambda i, j: (0, j))],
        out_specs=pl.BlockSpec((tm, tn), lambda i, j: (i, j)),
        out_shape=jax.ShapeDtypeStruct((m, n), BF16),
        compiler_params=_cparams(("parallel", "arbitrary")),
        name="branch_merge",
    )(yh, ya, z, z, wh, wa)


def _out_proj_kernel(x_ref, pre_ref, m_ref, w_ref, nw_ref, h1_ref, u2_ref, *, bps):
    is_prefix = pl.program_id(0) % bps == 0
    h = jnp.where(is_prefix, pre_ref[...], x_ref[...])
    h1 = h + jnp.dot(m_ref[...], w_ref[...], preferred_element_type=F32)
    h1_ref[...] = h1
    u2_ref[...] = _rms_rows(h1, nw_ref[...]).astype(u2_ref.dtype)


def _out_proj(x2, prefix, mm, w, nw, bsz, tp):
    m, d = mm.shape
    tm = SEQ_OFF
    bps = tp // tm
    row = lambda r: (r, 0)
    fixed = lambda r: (0, 0)
    return pl.pallas_call(
        functools.partial(_out_proj_kernel, bps=bps),
        grid=(m // tm,),
        in_specs=[pl.BlockSpec((tm, d), lambda r: (_x_block(r, bps), 0)), pl.BlockSpec((tm, d), fixed),
                  pl.BlockSpec((tm, d), row), pl.BlockSpec((d, d), fixed), pl.BlockSpec((1, d), fixed)],
        out_specs=[pl.BlockSpec((tm, d), row), pl.BlockSpec((tm, d), row)],
        out_shape=[jax.ShapeDtypeStruct((m, d), F32), jax.ShapeDtypeStruct((m, d), BF16)],
        compiler_params=_cparams(("parallel",)),
        name="out_proj",
    )(x2, prefix, mm, w, nw)


def _ffn_up_kernel(u_ref, wg_ref, wu_ref, cw_ref, cb_ref, o_ref, tail_ref, wg_scr, wu_scr):
    tm = u_ref.shape[0]

    @pl.when(pl.program_id(1) == 0)
    def _():
        tail_ref[...] = jnp.zeros_like(tail_ref)
        wg_scr[...] = wg_ref[...].astype(BF16)
        wu_scr[...] = wu_ref[...].astype(BF16)

    u = u_ref[...]
    g0 = jnp.dot(u, wg_scr[...], preferred_element_type=F32)
    up = jnp.dot(u, wu_scr[...], preferred_element_type=F32)
    row = lax.broadcasted_iota(jnp.int32, (tm, 1), 0)
    prev1 = tail_ref[7:8, :]
    prev2 = tail_ref[6:7, :]
    g1 = jnp.where(row == 0, prev1, pltpu.roll(g0, 1, axis=0))
    g2 = jnp.where(row == 0, prev2, jnp.where(row == 1, prev1, pltpu.roll(g0, 2, axis=0)))
    tail_ref[...] = g0[tm - 8:tm, :]
    a = cw_ref[0:1, :] * g2 + cw_ref[1:2, :] * g1 + cw_ref[2:3, :] * g0 + cb_ref[...]
    o_ref[...] = (a * jax.nn.sigmoid(a) * up).astype(o_ref.dtype)


def _ffn_up(u2, wg, wu, cw, cb, tm, tn):
    m, d = u2.shape
    n = wg.shape[1]
    return pl.pallas_call(
        _ffn_up_kernel,
        grid=(n // tn, m // tm),
        in_specs=[pl.BlockSpec((tm, d), lambda j, i: (i, 0)),
                  pl.BlockSpec((d, tn), lambda j, i: (0, j)),
                  pl.BlockSpec((d, tn), lambda j, i: (0, j)),
                  pl.BlockSpec((8, tn), lambda j, i: (0, j)),
                  pl.BlockSpec((1, tn), lambda j, i: (0, j))],
        out_specs=pl.BlockSpec((tm, tn), lambda j, i: (i, j)),
        out_shape=jax.ShapeDtypeStruct((m, n), BF16),
        scratch_shapes=[pltpu.VMEM((8, tn), F32), pltpu.VMEM((d, tn), BF16), pltpu.VMEM((d, tn), BF16)],
        compiler_params=_cparams(("parallel", "arbitrary")),
        name="ffn_up",
    )(u2, wg, wu, cw, cb)


def _ffn_down_kernel(g_ref, wd_ref, h1_ref, nw_ref, o_ref):
    h2 = h1_ref[...] + jnp.dot(g_ref[...], wd_ref[...], preferred_element_type=F32)
    o_ref[...] = _rms_rows(h2, nw_ref[...])


def _ffn_down(gact, wd, h1, nw, bsz, seq, tp, tm):
    m, kdim = gact.shape
    d = wd.shape[1]
    per_seq = seq // tm
    off = SEQ_OFF // tm

    def row_in(i):
        return ((i // per_seq) * (tp // tm) + off + i % per_seq, 0)

    return pl.pallas_call(
        _ffn_down_kernel,
        grid=(bsz * per_seq,),
        in_specs=[pl.BlockSpec((tm, kdim), row_in),
                  pl.BlockSpec((kdim, d), lambda i: (0, 0), pipeline_mode=pl.Buffered(1)),
                  pl.BlockSpec((tm, d), row_in),
                  pl.BlockSpec((1, d), lambda i: (0, 0))],
        out_specs=pl.BlockSpec((tm, d), lambda i: (i, 0)),
        out_shape=jax.ShapeDtypeStruct((bsz * seq, d), F32),
        compiler_params=_cparams(("parallel",)),
        name="ffn_down",
    )(gact, wd, h1, nw)


def kernel(x, meta_tokens, attn_norm_w, w_in, hgrn_lb_logits, hgrn_norm_w, idx_k_norm_w, w_branch_hgrn,
           w_branch_dsa, w_out, ffn_norm_w, w_ffn_gate, w_ffn_up, ffn_conv_w, ffn_conv_b, w_ffn_down,
           final_norm_w):
    bsz, seq, d = x.shape
    depth = w_in.shape[0]
    topk = min(IDX_TOPK_MAX, seq // 4)
    tp = SEQ_OFF + seq
    m = bsz * tp
    hg_w = HG_HEADS * HG_DIM
    at_w = AT_HEADS * AT_HDIM
    n_main_lo = 4 * hg_w + 3 * at_w + IDX_HEADS * IDX_HDIM
    n_small = IDX_HDIM + IDX_HEADS

    prefix = jnp.concatenate([jnp.zeros((META0, d), x.dtype), meta_tokens.astype(x.dtype)], axis=0)
    x2 = x.reshape(bsz * seq, d)
    lbs = jnp.cumsum(jax.nn.softmax(hgrn_lb_logits.astype(F32), axis=0), axis=0)

    assert depth == 1, "multi-layer stacks are not supported"
    out = None
    for l in range(depth):
        wt = jnp.swapaxes(w_in[l], 0, 1)
        w_small = jnp.pad(wt[n_main_lo:n_main_lo + n_small], ((0, LANES - n_small), (0, 0))).T.astype(BF16)
        knw = jnp.pad(idx_k_norm_w[l], (0, LANES - IDX_HDIM)).reshape(1, LANES)

        u, zs, kt, kb = _embed(x2, prefix, attn_norm_w[l].reshape(1, d), w_small, knw, bsz, tp)
        z = _in_proj(u, wt, n_main_lo, n_main_lo + n_small, 2 * d, tm=1024, tn=1024)
        z3 = z.reshape(bsz, tp, z.shape[1])

        y_h = _hgrn(z3, lbs[l].reshape(1, hg_w), hgrn_norm_w[l].reshape(1, HG_DIM))

        wt = zs.reshape(bsz, tp, LANES)[:, :, IDX_HDIM:IDX_HDIM + IDX_HEADS]
        wt = jnp.swapaxes(wt * (IDX_HEADS ** -0.5 * IDX_HDIM ** -0.5), 1, 2)
        bias = _dsa_index(z3, wt, kt.reshape(bsz, tp, LANES), kb.reshape(bsz, tp, LANES), topk)
        av = z3[:, :, 4 * hg_w + 2 * at_w:4 * hg_w + 3 * at_w]
        vt = jnp.swapaxes(av.reshape(bsz, tp // DSA_KC, DSA_KC, at_w), 2, 3)
        y_a = _dsa_attn(z3, vt, bias)

        mm = _merge(y_h.reshape(m, hg_w), y_a.reshape(m, at_w), z,
                    w_branch_hgrn[l].astype(BF16), w_branch_dsa[l].astype(BF16), tm=1024, tn=1024)
        h1, u2 = _out_proj(x2, prefix, mm, w_out[l].astype(BF16), ffn_norm_w[l].reshape(1, d), bsz, tp)

        cw = jnp.pad(ffn_conv_w[l], ((0, 8 - CONV_W), (0, 0)))
        gact = _ffn_up(u2, w_ffn_gate[l], w_ffn_up[l], cw,
                       ffn_conv_b[l].reshape(1, -1), tm=1024, tn=512)
        out = _ffn_down(gact, w_ffn_down[l].astype(BF16), h1, final_norm_w.reshape(1, d),
                        bsz, seq, tp, tm=256)
    return out.reshape(bsz, seq, d)
```

```python
import functools
import math

import jax
import jax.numpy as jnp
import numpy as np
from jax import lax
from jax.experimental import pallas as pl
from jax.experimental.pallas import tpu as pltpu

N_META = 16
EPS = 1e-6
HG_HEADS = 8
HG_DIM = 128
AT_HEADS = 8
AT_HDIM = 128
IDX_HEADS = 16
IDX_HDIM = 64
IDX_TOPK_MAX = 256
CONV_W = 3

LANES = 128
SUBLANES = 8
SEQ_OFF = 512
META0 = SEQ_OFF - N_META
NEG_BIG = -1e30
INT_MIN = -(2 ** 31)

HG_CHUNK = 128
HG_DIAG = 4
SCAN_STEPS = (1, 2, 4)
HG_STEP = 512
HG_HPS = 4
DSA_TQ = 256
DSA_KC = 512
COUNT_ROWS = 64
HALF_OFF = 2 ** 15
ATT_GROUPS = 1
REDUCE_ROWS = 64
ACC_ROWS = AT_HDIM + 16
VMEM_LIMIT = 56 * 1024 * 1024

F32 = jnp.float32
BF16 = jnp.bfloat16
NT_DIMS = (((1,), (1,)), ((), ()))
TN_DIMS = (((0,), (0,)), ((), ()))


def _cparams(sem):
    return pltpu.CompilerParams(dimension_semantics=sem, vmem_limit_bytes=VMEM_LIMIT)


def _rms_rows(x, w):
    ms = jnp.mean(x * x, axis=-1, keepdims=True)
    return x * lax.rsqrt(ms + EPS) * w


def _rows_reduce(x, reduce):
    rows, cols = x.shape
    if rows > REDUCE_ROWS:
        x = reduce(x.reshape(rows // REDUCE_ROWS, REDUCE_ROWS, cols), axis=0)
    return reduce(x, axis=0, keepdims=True)


def _x_block(r, bps):
    return (r // bps) * (bps - 1) + jnp.maximum(r % bps - 1, 0)


def _embed_kernel(x_ref, pre_ref, nw_ref, w_ref, knw_ref, u_ref, zs_ref, kt_ref, kb_ref, *, bps):
    is_prefix = pl.program_id(0) % bps == 0
    h = jnp.where(is_prefix, pre_ref[...], x_ref[...])
    u = _rms_rows(h, nw_ref[...]).astype(BF16)
    u_ref[...] = u
    zs = jnp.dot(u, w_ref[...], preferred_element_type=F32)
    lane = lax.broadcasted_iota(jnp.int32, zs.shape, 1)
    ik = jnp.where(lane < IDX_HDIM, zs, 0.0)
    ms = jnp.sum(ik * ik, axis=-1, keepdims=True) / IDX_HDIM
    kn = ik * lax.rsqrt(ms + EPS) * knw_ref[...]
    zs_ref[...] = zs
    kt_ref[...] = kn.astype(BF16)
    kb_ref[...] = pltpu.roll(kn, IDX_HDIM, axis=1).astype(BF16)


def _embed(x2, prefix, nw, w, knw, bsz, tp):
    d = x2.shape[1]
    tm = SEQ_OFF
    bps = tp // tm
    m = bsz * tp
    row = lambda r: (r, 0)
    fixed = lambda r: (0, 0)
    return pl.pallas_call(
        functools.partial(_embed_kernel, bps=bps),
        grid=(m // tm,),
        in_specs=[pl.BlockSpec((tm, d), lambda r: (_x_block(r, bps), 0)),
                  pl.BlockSpec((tm, d), fixed), pl.BlockSpec((1, d), fixed),
                  pl.BlockSpec((d, LANES), fixed), pl.BlockSpec((1, LANES), fixed)],
        out_specs=[pl.BlockSpec((tm, d), row)] + [pl.BlockSpec((tm, LANES), row)] * 3,
        out_shape=[jax.ShapeDtypeStruct((m, d), BF16),
                   jax.ShapeDtypeStruct((m, LANES), F32),
                   jax.ShapeDtypeStruct((m, LANES), BF16),
                   jax.ShapeDtypeStruct((m, LANES), BF16)],
        compiler_params=_cparams(("parallel",)),
        name="embed_norm",
    )(x2, prefix, nw, w, knw)


def _in_proj_kernel(u_ref, wt_ref, o_ref, w_scr):
    @pl.when(pl.program_id(1) == 0)
    def _():
        w_scr[...] = wt_ref[...].astype(BF16)

    o_ref[...] = lax.dot_general(u_ref[...], w_scr[...], NT_DIMS,
                                 preferred_element_type=F32).astype(o_ref.dtype)


def _in_proj(u, wt, n_lo_cols, hi_start, n_hi_cols, tm, tn):
    m, d = u.shape
    n_lo = n_lo_cols // tn
    n_blocks = n_lo + n_hi_cols // tn

    assert hi_start % SUBLANES == 0

    def w_rows(j, i):
        return (pl.multiple_of(jnp.where(j < n_lo, j * tn, hi_start + (j - n_lo) * tn), SUBLANES), 0)

    return pl.pallas_call(
        _in_proj_kernel,
        grid=(n_blocks, m // tm),
        in_specs=[pl.BlockSpec((tm, d), lambda j, i: (i, 0)),
                  pl.BlockSpec((pl.Element(tn), pl.Element(d)), w_rows)],
        out_specs=pl.BlockSpec((tm, tn), lambda j, i: (i, j)),
        out_shape=jax.ShapeDtypeStruct((m, n_blocks * tn), BF16),
        scratch_shapes=[pltpu.VMEM((tn, d), BF16)],
        compiler_params=_cparams(("parallel", "arbitrary")),
        name="in_proj",
    )(u, wt)


def _hgrn_masks():
    c = HG_CHUNK
    t = np.arange(c)[:, None]
    s = np.arange(c)[None, :]
    scan = [np.broadcast_to((t % 8) >= k, (c, c)) for k in SCAN_STEPS]
    levels = [(t // (2 * m) == s // (2 * m)) & ((t // m) % 2 == 1) & ((s // m) % 2 == 0) for m in _hgrn_levels()]
    diags = [(s == t - d) & ((t % HG_DIAG) >= d) for d in range(HG_DIAG)]
    return np.stack(scan + levels + diags).astype(np.float32)


def _hgrn_levels():
    m, out = HG_CHUNK // 2, []
    while m >= HG_DIAG:
        out.append(m)
        m //= 2
    return out


def _sigmoid(x):
    return 1.0 / (1.0 + jnp.exp(-x))


def _hgrn_kernel(q_ref, f_ref, i_ref, g_ref, lb_ref, nw_ref, mk_ref, o_ref, st_ref):
    c = HG_CHUNK
    levels = _hgrn_levels()
    n_scan = len(SCAN_STEPS)

    @pl.when(pl.program_id(2) == 0)
    def _():
        st_ref[...] = jnp.zeros_like(st_ref)

    def chunk(ci, hh, st):
        rows = pl.ds(ci * c, c)
        cols = slice(hh * HG_DIM, (hh + 1) * HG_DIM)
        lb = lb_ref[:, cols]
        oml = 1.0 - lb
        qz = q_ref[rows, cols].astype(F32)
        fz = f_ref[rows, cols].astype(F32)
        v = i_ref[rows, cols]
        gz = g_ref[rows, cols].astype(F32)

        q = qz * _sigmoid(qz)
        sf = _sigmoid(fz)
        f = lb + oml * sf
        kc = oml * (1.0 - sf)

        bw = jnp.log2(f)
        for k, step in enumerate(SCAN_STEPS):
            bw = bw + pltpu.roll(bw, step, axis=0) * mk_ref[k]
        run = jnp.zeros((1, HG_DIM), F32)
        tiles = []
        for i in range(c // 8):
            tiles.append(bw[8 * i:8 * i + 8, :] + run)
            run = run + bw[8 * i + 7:8 * i + 8, :]
        b = jnp.concatenate(tiles, axis=0)

        a = jnp.zeros((c, c), F32)
        for k, m in enumerate(levels):
            piv = jnp.concatenate(
                [jnp.broadcast_to(b[blk * 2 * m + m - 1: blk * 2 * m + m, :], (2 * m, HG_DIM))
                 for blk in range(c // (2 * m))], axis=0)
            x = jnp.exp2(-jnp.abs(b - piv))
            am = lax.dot_general((q * x).astype(BF16), (kc * x).astype(BF16), NT_DIMS,
                                 preferred_element_type=F32)
            a = a + am * mk_ref[n_scan + k]

        gd = None
        for d in range(HG_DIAG):
            if d == 0:
                xd = q * kc
            else:
                fd = f if d == 1 else pltpu.roll(f, d - 1, axis=0)
                gd = fd if gd is None else gd * fd
                xd = q * pltpu.roll(kc, d, axis=0) * gd
            a = a + jnp.sum(xd, axis=-1, keepdims=True) * mk_ref[n_scan + len(levels) + d]

        qd = (q * jnp.exp2(b)).astype(BF16)
        o = lax.dot_general(qd, st.astype(BF16), NT_DIMS, preferred_element_type=F32)
        o = o + jnp.dot(a.astype(BF16), v, preferred_element_type=F32)

        b_last = b[c - 1:c, :]
        kd = (kc * jnp.exp2(b_last - b)).astype(BF16)
        st = st * jnp.exp2(b_last) + lax.dot_general(v, kd, TN_DIMS, preferred_element_type=F32)

        y = _rms_rows(o, nw_ref[...]) * (gz * _sigmoid(gz))
        o_ref[rows, cols] = y.astype(o_ref.dtype)
        return st

    sts = [st_ref[hh] for hh in range(HG_HPS)]
    for ci in range(q_ref.shape[0] // c):
        for hh in range(HG_HPS):
            sts[hh] = chunk(ci, hh, sts[hh])
    for hh in range(HG_HPS):
        st_ref[hh] = sts[hh]


def _hgrn(z3, lb, nw):
    bsz, tp, _ = z3.shape
    masks = jnp.asarray(_hgrn_masks())

    hw = HG_HPS * HG_DIM
    hgroups = HG_HEADS // HG_HPS

    def zspec(group):
        return pl.BlockSpec((None, HG_STEP, hw), lambda b, h, s: (b, s, group * hgroups + h))

    return pl.pallas_call(
        _hgrn_kernel,
        grid=(bsz, hgroups, tp // HG_STEP),
        in_specs=[zspec(0), zspec(1), zspec(2), zspec(3),
                  pl.BlockSpec((1, hw), lambda b, h, s: (0, h)),
                  pl.BlockSpec((1, HG_DIM), lambda b, h, s: (0, 0)),
                  pl.BlockSpec(masks.shape, lambda b, h, s: (0, 0, 0))],
        out_specs=pl.BlockSpec((None, HG_STEP, hw), lambda b, h, s: (b, s, h)),
        out_shape=jax.ShapeDtypeStruct((bsz, tp, HG_HEADS * HG_DIM), BF16),
        scratch_shapes=[pltpu.VMEM((HG_HPS, HG_DIM, HG_DIM), F32)],
        compiler_params=_cparams(("parallel", "parallel", "arbitrary")),
        name="hgrn2",
    )(z3, z3, z3, z3, lb, nw, masks)


def _dsa_index_kernel(iq_ref, wt_ref, kt_ref, kb_ref, bias_ref, keys_ref, hi_ref, lo_ref, acc_ref, *, topk):
    tq = iq_ref.shape[0]
    kc = DSA_KC
    tp = kt_ref.shape[0]
    q0 = pl.program_id(1) * tq
    nchunks = (q0 + tq - 1) // kc + 1
    qpos = q0 + lax.broadcasted_iota(jnp.int32, (1, tq), 1)
    int_min = jnp.int32(INT_MIN)
    half = jnp.int32(HALF_OFF)

    def score_chunk(ci, carry):
        k0 = pl.multiple_of(ci * kc, kc)
        kt = kt_ref[pl.ds(k0, kc), :]
        kb = kb_ref[pl.ds(k0, kc), :]
        for p in range(IDX_HEADS // 2):
            qp = iq_ref[:, p * LANES:(p + 1) * LANES]
            st = lax.dot_general(kt, qp, NT_DIMS, preferred_element_type=F32)
            sb = lax.dot_general(kb, qp, NT_DIMS, preferred_element_type=F32)
            part = (wt_ref[2 * p:2 * p + 1, :] * jnp.maximum(st, 0.0)
                    + wt_ref[2 * p + 1:2 * p + 2, :] * jnp.maximum(sb, 0.0))
            if p == 0:
                acc_ref[...] = part
            else:
                acc_ref[...] += part
        score = acc_ref[...] + 0.0
        bits = lax.bitcast_convert_type(score, jnp.int32)
        key = bits ^ ((bits >> 31) & jnp.int32(0x7FFFFFFF))
        kpos = k0 + lax.broadcasted_iota(jnp.int32, (kc, 1), 0)
        valid = (kpos >= META0) & (kpos <= qpos)
        key = jnp.where(valid, key, int_min)
        keys_ref[pl.ds(k0, kc), :] = key
        hi_ref[pl.ds(k0, kc), :] = (key >> 16).astype(jnp.int16)
        lo_ref[pl.ds(k0, kc), :] = (((key << 16) ^ int_min) >> 16).astype(jnp.int16)
        return carry

    lax.fori_loop(0, nchunks, score_chunk, 0)

    def count_ge(src_ref, cand_u):
        cand = (cand_u - half).astype(jnp.int16)

        def count_chunk(ci, cnt):
            k0 = pl.multiple_of(ci * kc, kc)
            for r in range(kc // COUNT_ROWS):
                ks = src_ref[pl.ds(k0 + r * COUNT_ROWS, COUNT_ROWS), :]
                cnt = cnt + jnp.where(ks >= cand, jnp.int16(1), jnp.int16(0))
            return cnt

        cnt = lax.fori_loop(0, nchunks, count_chunk, jnp.zeros((COUNT_ROWS, tq), jnp.int16))
        return jnp.sum(cnt.astype(F32), axis=0, keepdims=True)

    def bisect16(src_ref, need):
        def bit_pass(i, u):
            cand_u = u | lax.shift_left(jnp.int32(1), 15 - i)
            return jnp.where(count_ge(src_ref, cand_u) >= need, cand_u, u)
        return lax.fori_loop(0, 16, bit_pass, jnp.zeros((1, tq), jnp.int32))

    u_hi = bisect16(hi_ref, jnp.full((1, tq), float(topk), F32))
    top = 2 * HALF_OFF - 1
    above = jnp.where(u_hi == top, 0.0, count_ge(hi_ref, jnp.minimum(u_hi + 1, top)))
    hi_thr = (u_hi - half).astype(jnp.int16)

    def narrow_chunk(ci, carry):
        rows = pl.ds(pl.multiple_of(ci * kc, kc), kc)
        lo_ref[rows, :] = jnp.where(hi_ref[rows, :] == hi_thr, lo_ref[rows, :], jnp.int16(-HALF_OFF))
        return carry

    lax.fori_loop(0, nchunks, narrow_chunk, 0)
    u_lo = bisect16(lo_ref, topk - above)
    thr = jnp.where(u_hi == 0, int_min + 1, ((u_hi - half) << 16) | u_lo)

    def write_chunk(ci, carry):
        k0 = pl.multiple_of(ci * kc, kc)
        ks = keys_ref[pl.ds(k0, kc), :]
        bias_ref[pl.ds(k0, kc), :] = jnp.where(ks >= thr, 0.0, NEG_BIG).astype(bias_ref.dtype)
        return carry

    lax.fori_loop(0, nchunks, write_chunk, 0)

    def fill_chunk(ci, carry):
        k0 = pl.multiple_of(ci * kc, kc)
        bias_ref[pl.ds(k0, kc), :] = jnp.full((kc, tq), NEG_BIG, bias_ref.dtype)
        return carry

    lax.fori_loop(nchunks, tp // kc, fill_chunk, 0)


def _dsa_index(z3, wt, kt, kb, topk):
    bsz, tp, _ = z3.shape
    iq_col = (HG_HEADS * HG_DIM * 4 + AT_HEADS * AT_HDIM * 3) // (IDX_HEADS * IDX_HDIM)
    return pl.pallas_call(
        functools.partial(_dsa_index_kernel, topk=topk),
        grid=(bsz, tp // DSA_TQ),
        in_specs=[pl.BlockSpec((None, DSA_TQ, IDX_HEADS * IDX_HDIM), lambda b, i: (b, i, iq_col)),
                  pl.BlockSpec((None, IDX_HEADS, DSA_TQ), lambda b, i: (b, 0, i)),
                  pl.BlockSpec((None, tp, LANES), lambda b, i: (b, 0, 0)),
                  pl.BlockSpec((None, tp, LANES), lambda b, i: (b, 0, 0))],
        out_specs=pl.BlockSpec((None, None, tp, DSA_TQ), lambda b, i: (b, i, 0, 0)),
        out_shape=jax.ShapeDtypeStruct((bsz, tp // DSA_TQ, tp, DSA_TQ), BF16),
        scratch_shapes=[pltpu.VMEM((tp, DSA_TQ), jnp.int32), pltpu.VMEM((tp, DSA_TQ), jnp.int16),
                        pltpu.VMEM((tp, DSA_TQ), jnp.int16), pltpu.VMEM((DSA_KC, DSA_TQ), F32)],
        compiler_params=_cparams(("parallel", "arbitrary")),
        name="dsa_index",
    )(z3, wt, kt, kb)


def _dsa_attn_kernel(aq_ref, k_ref, vt_ref, bias_ref, o_ref, qs_ref, acc_ref, lg_ref, p_ref):
    tq = aq_ref.shape[0]
    kc = DSA_KC
    heads = aq_ref.shape[1] // AT_HDIM
    q0 = pl.program_id(2) * tq
    nchunks = (q0 + tq - 1) // kc + 1

    qs_ref[...] = (aq_ref[...].astype(F32) * (AT_HDIM ** -0.5 * math.log2(math.e))).astype(qs_ref.dtype)
    acc_ref[...] = jnp.zeros(acc_ref.shape, F32)
    ones_rows = (lax.broadcasted_iota(jnp.int32, (ACC_ROWS - AT_HDIM, kc), 0) == 0).astype(BF16)

    def chunk(ci, ms):
        k0 = pl.multiple_of(ci * kc, kc)
        bias = bias_ref[pl.ds(k0, kc), :]
        new_ms = []
        for h in range(heads):
            hs = slice(h * AT_HDIM, (h + 1) * AT_HDIM)
            lg = lax.dot_general(k_ref[pl.ds(k0, kc), hs], qs_ref[:, hs], NT_DIMS,
                                 preferred_element_type=F32).astype(BF16) + bias
            lg_ref[h] = lg
            new_ms.append(jnp.maximum(ms[h], _rows_reduce(lg, jnp.max)))
        for h in range(heads):
            hs = slice(h * AT_HDIM, (h + 1) * AT_HDIM)
            alpha = jnp.exp2(ms[h].astype(F32) - new_ms[h].astype(F32))
            p_ref[h] = jnp.exp2(lg_ref[h] - new_ms[h])
            v_ext = jnp.concatenate([vt_ref[ci, hs, :], ones_rows], axis=0)
            acc_ref[h] = alpha * acc_ref[h] + jnp.dot(v_ext, p_ref[h], preferred_element_type=F32)
        return tuple(new_ms)

    init = tuple(jnp.full((1, tq), NEG_BIG, BF16) for _ in range(heads))
    lax.fori_loop(0, nchunks, chunk, init)

    live = (q0 + lax.broadcasted_iota(jnp.int32, (1, tq), 1)) >= META0
    for h in range(heads):
        num = acc_ref[h, 0:AT_HDIM, :]
        den = acc_ref[h, AT_HDIM:AT_HDIM + 1, :]
        out_t = jnp.where(live, num / den, 0.0)
        o_ref[:, h * AT_HDIM:(h + 1) * AT_HDIM] = out_t.T.astype(o_ref.dtype)


def _dsa_attn(z3, vt, bias):
    bsz, tp, _ = z3.shape
    gw = AT_HEADS * AT_HDIM // ATT_GROUPS
    aq_col = HG_HEADS * HG_DIM * 4 // gw
    ak_col = aq_col + ATT_GROUPS
    hpg = AT_HEADS // ATT_GROUPS
    return pl.pallas_call(
        _dsa_attn_kernel,
        grid=(bsz, ATT_GROUPS, tp // DSA_TQ),
        in_specs=[pl.BlockSpec((None, DSA_TQ, gw), lambda b, g, i: (b, i, aq_col + g)),
                  pl.BlockSpec((None, tp, gw), lambda b, g, i: (b, 0, ak_col + g), pipeline_mode=pl.Buffered(1)),
                  pl.BlockSpec((None, tp // DSA_KC, gw, DSA_KC), lambda b, g, i: (b, 0, g, 0),
                               pipeline_mode=pl.Buffered(1)),
                  pl.BlockSpec((None, None, tp, DSA_TQ), lambda b, g, i: (b, i, 0, 0))],
        out_specs=pl.BlockSpec((None, DSA_TQ, gw), lambda b, g, i: (b, i, g)),
        out_shape=jax.ShapeDtypeStruct((bsz, tp, AT_HEADS * AT_HDIM), BF16),
        scratch_shapes=[pltpu.VMEM((DSA_TQ, gw), BF16),
                        pltpu.VMEM((hpg, ACC_ROWS, DSA_TQ), F32),
                        pltpu.VMEM((hpg, DSA_KC, DSA_TQ), BF16),
                        pltpu.VMEM((hpg, DSA_KC, DSA_TQ), BF16)],
        compiler_params=_cparams(("parallel", "parallel", "arbitrary")),
        name="dsa_attn",
    )(z3, z3, vt, bias)


def _merge_kernel(yh_ref, ya_ref, ga_ref, gb_ref, wh_ref, wa_ref, o_ref):
    mh = jnp.dot(yh_ref[...], wh_ref[...], preferred_element_type=F32)
    ma = jnp.dot(ya_ref[...], wa_ref[...], preferred_element_type=F32)
    o = jax.nn.sigmoid(ga_ref[...].astype(F32)) * mh + jax.nn.sigmoid(gb_ref[...].astype(F32)) * ma
    o_ref[...] = o.astype(o_ref.dtype)


def _merge(yh, ya, z, wh, wa, tm, tn):
    m, kdim = yh.shape
    n = wh.shape[1]
    ga_col = (z.shape[1] - 2 * n) // tn
    gb_col = (z.shape[1] - n) // tn
    return pl.pallas_call(
        _merge_kernel,
        grid=(m // tm, n // tn),
        in_specs=[pl.BlockSpec((tm, kdim), lambda i, j: (i, 0)),
                  pl.BlockSpec((tm, kdim), lambda i, j: (i, 0)),
                  pl.BlockSpec((tm, tn), lambda i, j: (i, ga_col + j)),
                  pl.BlockSpec((tm, tn), lambda i, j: (i, gb_col + j)),
                  pl.BlockSpec((kdim, tn), lambda i, j: (0, j)),
                  pl.BlockSpec((kdim, tn), lambda i, j: (0, j))],
        out_specs=pl.BlockSpec((tm, tn), lambda i, j: (i, j)),
        out_shape=jax.ShapeDtypeStruct((m, n), BF16),
        compiler_params=_cparams(("parallel", "arbitrary")),
        name="branch_merge",
    )(yh, ya, z, z, wh, wa)


def _out_proj_kernel(x_ref, pre_ref, m_ref, w_ref, nw_ref, h1_ref, u2_ref, *, bps):
    is_prefix = pl.program_id(0) % bps == 0
    h = jnp.where(is_prefix, pre_ref[...], x_ref[...])
    h1 = h + jnp.dot(m_ref[...], w_ref[...], preferred_element_type=F32)
    h1_ref[...] = h1
    u2_ref[...] = _rms_rows(h1, nw_ref[...]).astype(u2_ref.dtype)


def _out_proj(x2, prefix, mm, w, nw, bsz, tp):
    m, d = mm.shape
    tm = SEQ_OFF
    bps = tp // tm
    row = lambda r: (r, 0)
    fixed = lambda r: (0, 0)
    return pl.pallas_call(
        functools.partial(_out_proj_kernel, bps=bps),
        grid=(m // tm,),
        in_specs=[pl.BlockSpec((tm, d), lambda r: (_x_block(r, bps), 0)), pl.BlockSpec((tm, d), fixed),
                  pl.BlockSpec((tm, d), row), pl.BlockSpec((d, d), fixed), pl.BlockSpec((1, d), fixed)],
        out_specs=[pl.BlockSpec((tm, d), row), pl.BlockSpec((tm, d), row)],
        out_shape=[jax.ShapeDtypeStruct((m, d), F32), jax.ShapeDtypeStruct((m, d), BF16)],
        compiler_params=_cparams(("parallel",)),
        name="out_proj",
    )(x2, prefix, mm, w, nw)


def _ffn_up_kernel(u_ref, wg_ref, wu_ref, cw_ref, cb_ref, o_ref, tail_ref, wg_scr, wu_scr):
    tm = u_ref.shape[0]

    @pl.when(pl.program_id(1) == 0)
    def _():
        tail_ref[...] = jnp.zeros_like(tail_ref)
        wg_scr[...] = wg_ref[...].astype(BF16)
        wu_scr[...] = wu_ref[...].astype(BF16)

    u = u_ref[...]
    g0 = jnp.dot(u, wg_scr[...], preferred_element_type=F32)
    up = jnp.dot(u, wu_scr[...], preferred_element_type=F32)
    row = lax.broadcasted_iota(jnp.int32, (tm, 1), 0)
    prev1 = tail_ref[7:8, :]
    prev2 = tail_ref[6:7, :]
    g1 = jnp.where(row == 0, prev1, pltpu.roll(g0, 1, axis=0))
    g2 = jnp.where(row == 0, prev2, jnp.where(row == 1, prev1, pltpu.roll(g0, 2, axis=0)))
    tail_ref[...] = g0[tm - 8:tm, :]
    a = cw_ref[0:1, :] * g2 + cw_ref[1:2, :] * g1 + cw_ref[2:3, :] * g0 + cb_ref[...]
    o_ref[...] = (a * jax.nn.sigmoid(a) * up).astype(o_ref.dtype)


def _ffn_up(u2, wg, wu, cw, cb, tm, tn):
    m, d = u2.shape
    n = wg.shape[1]
    return pl.pallas_call(
        _ffn_up_kernel,
        grid=(n // tn, m // tm),
        in_specs=[pl.BlockSpec((tm, d), lambda j, i: (i, 0)),
                  pl.BlockSpec((d, tn), lambda j, i: (0, j)),
                  pl.BlockSpec((d, tn), lambda j, i: (0, j)),
                  pl.BlockSpec((8, tn), lambda j, i: (0, j)),
                  pl.BlockSpec((1, tn), lambda j, i: (0, j))],
        out_specs=pl.BlockSpec((tm, tn), lambda j, i: (i, j)),
        out_shape=jax.ShapeDtypeStruct((m, n), BF16),
        scratch_shapes=[pltpu.VMEM((8, tn), F32), pltpu.VMEM((d, tn), BF16), pltpu.VMEM((d, tn), BF16)],
        compiler_params=_cparams(("parallel", "arbitrary")),
        name="ffn_up",
    )(u2, wg, wu, cw, cb)


def _ffn_down_kernel(g_ref, wd_ref, h1_ref, nw_ref, o_ref):
    h2 = h1_ref[...] + jnp.dot(g_ref[...], wd_ref[...], preferred_element_type=F32)
    o_ref[...] = _rms_rows(h2, nw_ref[...])


def _ffn_down(gact, wd, h1, nw, bsz, seq, tp, tm):
    m, kdim = gact.shape
    d = wd.shape[1]
    per_seq = seq // tm
    off = SEQ_OFF // tm

    def row_in(i):
        return ((i // per_seq) * (tp // tm) + off + i % per_seq, 0)

    return pl.pallas_call(
        _ffn_down_kernel,
        grid=(bsz * per_seq,),
        in_specs=[pl.BlockSpec((tm, kdim), row_in),
                  pl.BlockSpec((kdim, d), lambda i: (0, 0), pipeline_mode=pl.Buffered(1)),
                  pl.BlockSpec((tm, d), row_in),
                  pl.BlockSpec((1, d), lambda i: (0, 0))],
        out_specs=pl.BlockSpec((tm, d), lambda i: (i, 0)),
        out_shape=jax.ShapeDtypeStruct((bsz * seq, d), F32),
        compiler_params=_cparams(("parallel",)),
        name="ffn_down",
    )(gact, wd, h1, nw)


def kernel(x, meta_tokens, attn_norm_w, w_in, hgrn_lb_logits, hgrn_norm_w, idx_k_norm_w, w_branch_hgrn,
           w_branch_dsa, w_out, ffn_norm_w, w_ffn_gate, w_ffn_up, ffn_conv_w, ffn_conv_b, w_ffn_down,
           final_norm_w):
    bsz, seq, d = x.shape
    depth = w_in.shape[0]
    topk = min(IDX_TOPK_MAX, seq // 4)
    tp = SEQ_OFF + seq
    m = bsz * tp
    hg_w = HG_HEADS * HG_DIM
    at_w = AT_HEADS * AT_HDIM
    n_main_lo = 4 * hg_w + 3 * at_w + IDX_HEADS * IDX_HDIM
    n_small = IDX_HDIM + IDX_HEADS

    prefix = jnp.concatenate([jnp.zeros((META0, d), x.dtype), meta_tokens.astype(x.dtype)], axis=0)
    x2 = x.reshape(bsz * seq, d)
    lbs = jnp.cumsum(jax.nn.softmax(hgrn_lb_logits.astype(F32), axis=0), axis=0)

    assert depth == 1, "multi-layer stacks are not supported"
    out = None
    for l in range(depth):
        wt = jnp.swapaxes(w_in[l], 0, 1)
        w_small = jnp.pad(wt[n_main_lo:n_main_lo + n_small], ((0, LANES - n_small), (0, 0))).T.astype(BF16)
        knw = jnp.pad(idx_k_norm_w[l], (0, LANES - IDX_HDIM)).reshape(1, LANES)

        u, zs, kt, kb = _embed(x2, prefix, attn_norm_w[l].reshape(1, d), w_small, knw, bsz, tp)
        z = _in_proj(u, wt, n_main_lo, n_main_lo + n_small, 2 * d, tm=1024, tn=1024)
        z3 = z.reshape(bsz, tp, z.shape[1])

        y_h = _hgrn(z3, lbs[l].reshape(1, hg_w), hgrn_norm_w[l].reshape(1, HG_DIM))

        wt = zs.reshape(bsz, tp, LANES)[:, :, IDX_HDIM:IDX_HDIM + IDX_HEADS]
        wt = jnp.swapaxes(wt * (IDX_HEADS ** -0.5 * IDX_HDIM ** -0.5), 1, 2)
        bias = _dsa_index(z3, wt, kt.reshape(bsz, tp, LANES), kb.reshape(bsz, tp, LANES), topk)
        av = z3[:, :, 4 * hg_w + 2 * at_w:4 * hg_w + 3 * at_w]
        vt = jnp.swapaxes(av.reshape(bsz, tp // DSA_KC, DSA_KC, at_w), 2, 3)
        y_a = _dsa_attn(z3, vt, bias)

        mm = _merge(y_h.reshape(m, hg_w), y_a.reshape(m, at_w), z,
                    w_branch_hgrn[l].astype(BF16), w_branch_dsa[l].astype(BF16), tm=1024, tn=1024)
        h1, u2 = _out_proj(x2, prefix, mm, w_out[l].astype(BF16), ffn_norm_w[l].reshape(1, d), bsz, tp)

        cw = jnp.pad(ffn_conv_w[l], ((0, 8 - CONV_W), (0, 0)))
        gact = _ffn_up(u2, w_ffn_gate[l], w_ffn_up[l], cw,
                       ffn_conv_b[l].reshape(1, -1), tm=1024, tn=512)
        out = _ffn_down(gact, w_ffn_down[l].astype(BF16), h1, final_norm_w.reshape(1, d),
                        bsz, seq, tp, tm=256)
    return out.reshape(bsz, seq, d)
```

```python
import functools
import math

import jax
import jax.numpy as jnp
import numpy as np
from jax import lax
from jax.experimental import pallas as pl
from jax.experimental.pallas import tpu as pltpu

N_META = 16
EPS = 1e-6
HG_HEADS = 8
HG_DIM = 128
AT_HEADS = 8
AT_HDIM = 128
IDX_HEADS = 16
IDX_HDIM = 64
IDX_TOPK_MAX = 256
CONV_W = 3

LANES = 128
SUBLANES = 8
SEQ_OFF = 512
META0 = SEQ_OFF - N_META
NEG_BIG = -1e30
INT_MIN = -(2 ** 31)

HG_CHUNK = 128
HG_DIAG = 4
SCAN_STEPS = (1, 2, 4)
HG_STEP = 512
HG_HPS = 4
DSA_TQ = 256
DSA_KC = 512
COUNT_ROWS = 64
HALF_OFF = 2 ** 15
ATT_GROUPS = 1
REDUCE_ROWS = 64
ACC_ROWS = AT_HDIM + 16
MM_ROWS = 1024
MM_COLS = 1024
FFN_UP_COLS = 512
FFN_DOWN_ROWS = 256
VMEM_LIMIT = 56 * 1024 * 1024

F32 = jnp.float32
BF16 = jnp.bfloat16
NT_DIMS = (((1,), (1,)), ((), ()))
TN_DIMS = (((0,), (0,)), ((), ()))


def _cparams(sem):
    return pltpu.CompilerParams(dimension_semantics=sem, vmem_limit_bytes=VMEM_LIMIT)


def _rms_rows(x, w):
    ms = jnp.mean(x * x, axis=-1, keepdims=True)
    return x * lax.rsqrt(ms + EPS) * w


def _rows_reduce(x, reduce):
    rows, cols = x.shape
    if rows > REDUCE_ROWS:
        x = reduce(x.reshape(rows // REDUCE_ROWS, REDUCE_ROWS, cols), axis=0)
    return reduce(x, axis=0, keepdims=True)


def _x_block(r, bps):
    return (r // bps) * (bps - 1) + jnp.maximum(r % bps - 1, 0)


def _embed_kernel(x_ref, pre_ref, nw_ref, w_ref, knw_ref, u_ref, zs_ref, kt_ref, kb_ref, *, bps):
    is_prefix = pl.program_id(0) % bps == 0
    h = jnp.where(is_prefix, pre_ref[...], x_ref[...])
    u = _rms_rows(h, nw_ref[...]).astype(BF16)
    u_ref[...] = u
    zs = jnp.dot(u, w_ref[...], preferred_element_type=F32)
    lane = lax.broadcasted_iota(jnp.int32, zs.shape, 1)
    ik = jnp.where(lane < IDX_HDIM, zs, 0.0)
    ms = jnp.sum(ik * ik, axis=-1, keepdims=True) / IDX_HDIM
    kn = ik * lax.rsqrt(ms + EPS) * knw_ref[...]
    zs_ref[...] = zs
    kt_ref[...] = kn.astype(BF16)
    kb_ref[...] = pltpu.roll(kn, IDX_HDIM, axis=1).astype(BF16)


def _embed(x2, prefix, nw, w, knw, bsz, tp):
    d = x2.shape[1]
    tm = SEQ_OFF
    bps = tp // tm
    m = bsz * tp
    row = lambda r: (r, 0)
    fixed = lambda r: (0, 0)
    return pl.pallas_call(
        functools.partial(_embed_kernel, bps=bps),
        grid=(m // tm,),
        in_specs=[pl.BlockSpec((tm, d), lambda r: (_x_block(r, bps), 0)),
                  pl.BlockSpec((tm, d), fixed), pl.BlockSpec((1, d), fixed),
                  pl.BlockSpec((d, LANES), fixed), pl.BlockSpec((1, LANES), fixed)],
        out_specs=[pl.BlockSpec((tm, d), row)] + [pl.BlockSpec((tm, LANES), row)] * 3,
        out_shape=[jax.ShapeDtypeStruct((m, d), BF16),
                   jax.ShapeDtypeStruct((m, LANES), F32),
                   jax.ShapeDtypeStruct((m, LANES), BF16),
                   jax.ShapeDtypeStruct((m, LANES), BF16)],
        compiler_params=_cparams(("parallel",)),
        name="embed_norm",
    )(x2, prefix, nw, w, knw)


def _in_proj_kernel(u_ref, wt_ref, o_ref, w_scr):
    @pl.when(pl.program_id(1) == 0)
    def _():
        w_scr[...] = wt_ref[...].astype(BF16)

    o_ref[...] = lax.dot_general(u_ref[...], w_scr[...], NT_DIMS,
                                 preferred_element_type=F32).astype(o_ref.dtype)


def _in_proj(u, wt, n_lo_cols, hi_start, n_hi_cols, tm, tn):
    m, d = u.shape
    n_lo = n_lo_cols // tn
    n_blocks = n_lo + n_hi_cols // tn

    assert hi_start % SUBLANES == 0

    def w_rows(j, i):
        return (pl.multiple_of(jnp.where(j < n_lo, j * tn, hi_start + (j - n_lo) * tn), SUBLANES), 0)

    return pl.pallas_call(
        _in_proj_kernel,
        grid=(n_blocks, m // tm),
        in_specs=[pl.BlockSpec((tm, d), lambda j, i: (i, 0)),
                  pl.BlockSpec((pl.Element(tn), pl.Element(d)), w_rows)],
        out_specs=pl.BlockSpec((tm, tn), lambda j, i: (i, j)),
        out_shape=jax.ShapeDtypeStruct((m, n_blocks * tn), BF16),
        scratch_shapes=[pltpu.VMEM((tn, d), BF16)],
        compiler_params=_cparams(("parallel", "arbitrary")),
        name="in_proj",
    )(u, wt)


def _hgrn_masks():
    c = HG_CHUNK
    t = np.arange(c)[:, None]
    s = np.arange(c)[None, :]
    scan = [np.broadcast_to((t % 8) >= k, (c, c)) for k in SCAN_STEPS]
    levels = [(t // (2 * m) == s // (2 * m)) & ((t // m) % 2 == 1) & ((s // m) % 2 == 0) for m in _hgrn_levels()]
    diags = [(s == t - d) & ((t % HG_DIAG) >= d) for d in range(HG_DIAG)]
    return np.stack(scan + levels + diags).astype(np.float32)


def _hgrn_levels():
    m, out = HG_CHUNK // 2, []
    while m >= HG_DIAG:
        out.append(m)
        m //= 2
    return out


def _sigmoid(x):
    return 1.0 / (1.0 + jnp.exp(-x))


def _hgrn_kernel(q_ref, f_ref, i_ref, g_ref, lb_ref, nw_ref, mk_ref, o_ref, st_ref):
    c = HG_CHUNK
    levels = _hgrn_levels()
    n_scan = len(SCAN_STEPS)

    @pl.when(pl.program_id(2) == 0)
    def _():
        st_ref[...] = jnp.zeros_like(st_ref)

    def chunk(ci, hh, st):
        rows = pl.ds(ci * c, c)
        cols = slice(hh * HG_DIM, (hh + 1) * HG_DIM)
        lb = lb_ref[:, cols]
        oml = 1.0 - lb
        qz = q_ref[rows, cols].astype(F32)
        fz = f_ref[rows, cols].astype(F32)
        v = i_ref[rows, cols]
        gz = g_ref[rows, cols].astype(F32)

        q = qz * _sigmoid(qz)
        sf = _sigmoid(fz)
        f = lb + oml * sf
        kc = oml * (1.0 - sf)

        bw = jnp.log2(f)
        for k, step in enumerate(SCAN_STEPS):
            bw = bw + pltpu.roll(bw, step, axis=0) * mk_ref[k]
        run = jnp.zeros((1, HG_DIM), F32)
        tiles = []
        for i in range(c // 8):
            tiles.append(bw[8 * i:8 * i + 8, :] + run)
            run = run + bw[8 * i + 7:8 * i + 8, :]
        b = jnp.concatenate(tiles, axis=0)

        a = jnp.zeros((c, c), F32)
        for k, m in enumerate(levels):
            piv = jnp.concatenate(
                [jnp.broadcast_to(b[blk * 2 * m + m - 1: blk * 2 * m + m, :], (2 * m, HG_DIM))
                 for blk in range(c // (2 * m))], axis=0)
            x = jnp.exp2(-jnp.abs(b - piv))
            am = lax.dot_general((q * x).astype(BF16), (kc * x).astype(BF16), NT_DIMS,
                                 preferred_element_type=F32)
            a = a + am * mk_ref[n_scan + k]

        gd = None
        for d in range(HG_DIAG):
            if d == 0:
                xd = q * kc
            else:
                fd = f if d == 1 else pltpu.roll(f, d - 1, axis=0)
                gd = fd if gd is None else gd * fd
                xd = q * pltpu.roll(kc, d, axis=0) * gd
            a = a + jnp.sum(xd, axis=-1, keepdims=True) * mk_ref[n_scan + len(levels) + d]

        qd = (q * jnp.exp2(b)).astype(BF16)
        o = lax.dot_general(qd, st.astype(BF16), NT_DIMS, preferred_element_type=F32)
        o = o + jnp.dot(a.astype(BF16), v, preferred_element_type=F32)

        b_last = b[c - 1:c, :]
        kd = (kc * jnp.exp2(b_last - b)).astype(BF16)
        st = st * jnp.exp2(b_last) + lax.dot_general(v, kd, TN_DIMS, preferred_element_type=F32)

        y = _rms_rows(o, nw_ref[...]) * (gz * _sigmoid(gz))
        o_ref[rows, cols] = y.astype(o_ref.dtype)
        return st

    sts = [st_ref[hh] for hh in range(HG_HPS)]
    for ci in range(q_ref.shape[0] // c):
        for hh in range(HG_HPS):
            sts[hh] = chunk(ci, hh, sts[hh])
    for hh in range(HG_HPS):
        st_ref[hh] = sts[hh]


def _hgrn(z3, lb, nw):
    bsz, tp, _ = z3.shape
    masks = jnp.asarray(_hgrn_masks())

    hw = HG_HPS * HG_DIM
    hgroups = HG_HEADS // HG_HPS

    def zspec(group):
        return pl.BlockSpec((None, HG_STEP, hw), lambda b, h, s: (b, s, group * hgroups + h))

    return pl.pallas_call(
        _hgrn_kernel,
        grid=(bsz, hgroups, tp // HG_STEP),
        in_specs=[zspec(0), zspec(1), zspec(2), zspec(3),
                  pl.BlockSpec((1, hw), lambda b, h, s: (0, h)),
                  pl.BlockSpec((1, HG_DIM), lambda b, h, s: (0, 0)),
                  pl.BlockSpec(masks.shape, lambda b, h, s: (0, 0, 0))],
        out_specs=pl.BlockSpec((None, HG_STEP, hw), lambda b, h, s: (b, s, h)),
        out_shape=jax.ShapeDtypeStruct((bsz, tp, HG_HEADS * HG_DIM), BF16),
        scratch_shapes=[pltpu.VMEM((HG_HPS, HG_DIM, HG_DIM), F32)],
        compiler_params=_cparams(("parallel", "parallel", "arbitrary")),
        name="hgrn2",
    )(z3, z3, z3, z3, lb, nw, masks)


def _dsa_index_kernel(iq_ref, wt_ref, kt_ref, kb_ref, bias_ref, keys_ref, hi_ref, lo_ref, acc_ref, *, topk):
    tq = iq_ref.shape[0]
    kc = DSA_KC
    tp = kt_ref.shape[0]
    q0 = pl.program_id(1) * tq
    nchunks = (q0 + tq - 1) // kc + 1
    qpos = q0 + lax.broadcasted_iota(jnp.int32, (1, tq), 1)
    int_min = jnp.int32(INT_MIN)
    half = jnp.int32(HALF_OFF)

    def score_chunk(ci, carry):
        k0 = pl.multiple_of(ci * kc, kc)
        kt = kt_ref[pl.ds(k0, kc), :]
        kb = kb_ref[pl.ds(k0, kc), :]
        for p in range(IDX_HEADS // 2):
            qp = iq_ref[:, p * LANES:(p + 1) * LANES]
            st = lax.dot_general(kt, qp, NT_DIMS, preferred_element_type=F32)
            sb = lax.dot_general(kb, qp, NT_DIMS, preferred_element_type=F32)
            part = (wt_ref[2 * p:2 * p + 1, :] * jnp.maximum(st, 0.0)
                    + wt_ref[2 * p + 1:2 * p + 2, :] * jnp.maximum(sb, 0.0))
            if p == 0:
                acc_ref[...] = part
            else:
                acc_ref[...] += part
        score = acc_ref[...] + 0.0
        bits = lax.bitcast_convert_type(score, jnp.int32)
        key = bits ^ ((bits >> 31) & jnp.int32(0x7FFFFFFF))
        kpos = k0 + lax.broadcasted_iota(jnp.int32, (kc, 1), 0)
        valid = (kpos >= META0) & (kpos <= qpos)
        key = jnp.where(valid, key, int_min)
        keys_ref[pl.ds(k0, kc), :] = key
        hi_ref[pl.ds(k0, kc), :] = (key >> 16).astype(jnp.int16)
        lo_ref[pl.ds(k0, kc), :] = (((key << 16) ^ int_min) >> 16).astype(jnp.int16)
        return carry

    lax.fori_loop(0, nchunks, score_chunk, 0)

    def count_ge(src_ref, cand_u):
        cand = (cand_u - half).astype(jnp.int16)

        def count_chunk(ci, cnt):
            k0 = pl.multiple_of(ci * kc, kc)
            for r in range(kc // COUNT_ROWS):
                ks = src_ref[pl.ds(k0 + r * COUNT_ROWS, COUNT_ROWS), :]
                cnt = cnt + jnp.where(ks >= cand, jnp.int16(1), jnp.int16(0))
            return cnt

        cnt = lax.fori_loop(0, nchunks, count_chunk, jnp.zeros((COUNT_ROWS, tq), jnp.int16))
        return jnp.sum(cnt.astype(F32), axis=0, keepdims=True)

    def bisect16(src_ref, need, count_at_zero):
        def bit_pass(i, carry):
            u, at_u, above_u = carry
            cand_u = u | lax.shift_left(jnp.int32(1), 15 - i)
            c = count_ge(src_ref, cand_u)
            ok = c >= need
            return jnp.where(ok, cand_u, u), jnp.where(ok, c, at_u), jnp.where(ok, above_u, c)
        init = (jnp.zeros((1, tq), jnp.int32), count_at_zero, jnp.zeros((1, tq), F32))
        return lax.fori_loop(0, 16, bit_pass, init)

    kf = jnp.full((1, tq), float(topk), F32)
    u_hi, n_hi, above = bisect16(hi_ref, kf, kf)
    hi_thr = (u_hi - half).astype(jnp.int16)

    def narrow_chunk(ci, carry):
        rows = pl.ds(pl.multiple_of(ci * kc, kc), kc)
        lo_ref[rows, :] = jnp.where(hi_ref[rows, :] == hi_thr, lo_ref[rows, :], jnp.int16(-HALF_OFF))
        return carry

    lax.fori_loop(0, nchunks, narrow_chunk, 0)
    u_lo, n_lo, _ = bisect16(lo_ref, topk - above, n_hi - above)
    few = u_hi == 0
    thr = jnp.where(few, int_min + 1, ((u_hi - half) << 16) | u_lo)
    tied = jnp.where(few, 0.0, above + n_lo - topk)
    any_tied = jnp.max(tied) > 0.0

    @pl.when(jnp.logical_not(any_tied))
    def _():
        def write_chunk(ci, carry):
            k0 = pl.multiple_of(ci * kc, kc)
            ks = keys_ref[pl.ds(k0, kc), :]
            bias_ref[pl.ds(k0, kc), :] = jnp.where(ks >= thr, 0.0, NEG_BIG).astype(bias_ref.dtype)
            return carry

        lax.fori_loop(0, nchunks, write_chunk, 0)

    @pl.when(any_tied)
    def _():
        def count_gt(ci, cnt):
            k0 = pl.multiple_of(ci * kc, kc)
            return cnt + jnp.sum(jnp.where(keys_ref[pl.ds(k0, kc), :] > thr, 1.0, 0.0), axis=0, keepdims=True)

        room = topk - lax.fori_loop(0, nchunks, count_gt, jnp.zeros((1, tq), F32))
        lower_tri = (lax.broadcasted_iota(jnp.int32, (kc, kc), 0)
                     >= lax.broadcasted_iota(jnp.int32, (kc, kc), 1)).astype(BF16)

        def write_chunk(ci, seen):
            k0 = pl.multiple_of(ci * kc, kc)
            ks = keys_ref[pl.ds(k0, kc), :]
            eq = ks == thr
            rank = seen + jnp.dot(lower_tri, jnp.where(eq, 1.0, 0.0).astype(BF16), preferred_element_type=F32)
            sel = (ks > thr) | (eq & (rank <= room))
            bias_ref[pl.ds(k0, kc), :] = jnp.where(sel, 0.0, NEG_BIG).astype(bias_ref.dtype)
            return rank[kc - 1:kc, :]

        lax.fori_loop(0, nchunks, write_chunk, jnp.zeros((1, tq), F32))

    def fill_chunk(ci, carry):
        k0 = pl.multiple_of(ci * kc, kc)
        bias_ref[pl.ds(k0, kc), :] = jnp.full((kc, tq), NEG_BIG, bias_ref.dtype)
        return carry

    lax.fori_loop(nchunks, tp // kc, fill_chunk, 0)


def _dsa_index(z3, wt, kt, kb, topk):
    bsz, tp, _ = z3.shape
    iq_col = (HG_HEADS * HG_DIM * 4 + AT_HEADS * AT_HDIM * 3) // (IDX_HEADS * IDX_HDIM)
    return pl.pallas_call(
        functools.partial(_dsa_index_kernel, topk=topk),
        grid=(bsz, tp // DSA_TQ),
        in_specs=[pl.BlockSpec((None, DSA_TQ, IDX_HEADS * IDX_HDIM), lambda b, i: (b, i, iq_col)),
                  pl.BlockSpec((None, IDX_HEADS, DSA_TQ), lambda b, i: (b, 0, i)),
                  pl.BlockSpec((None, tp, LANES), lambda b, i: (b, 0, 0)),
                  pl.BlockSpec((None, tp, LANES), lambda b, i: (b, 0, 0))],
        out_specs=pl.BlockSpec((None, None, tp, DSA_TQ), lambda b, i: (b, i, 0, 0)),
        out_shape=jax.ShapeDtypeStruct((bsz, tp // DSA_TQ, tp, DSA_TQ), BF16),
        scratch_shapes=[pltpu.VMEM((tp, DSA_TQ), jnp.int32), pltpu.VMEM((tp, DSA_TQ), jnp.int16),
                        pltpu.VMEM((tp, DSA_TQ), jnp.int16), pltpu.VMEM((DSA_KC, DSA_TQ), F32)],
        compiler_params=_cparams(("parallel", "arbitrary")),
        name="dsa_index",
    )(z3, wt, kt, kb)


def _dsa_attn_kernel(aq_ref, k_ref, vt_ref, bias_ref, o_ref, qs_ref, acc_ref, lg_ref, p_ref):
    tq = aq_ref.shape[0]
    kc = DSA_KC
    heads = aq_ref.shape[1] // AT_HDIM
    q0 = pl.program_id(2) * tq
    nchunks = (q0 + tq - 1) // kc + 1

    qs_ref[...] = (aq_ref[...].astype(F32) * (AT_HDIM ** -0.5 * math.log2(math.e))).astype(qs_ref.dtype)
    acc_ref[...] = jnp.zeros(acc_ref.shape, F32)
    ones_rows = (lax.broadcasted_iota(jnp.int32, (ACC_ROWS - AT_HDIM, kc), 0) == 0).astype(BF16)

    def chunk(ci, ms):
        k0 = pl.multiple_of(ci * kc, kc)
        bias = bias_ref[pl.ds(k0, kc), :]
        new_ms = []
        for h in range(heads):
            hs = slice(h * AT_HDIM, (h + 1) * AT_HDIM)
            lg = lax.dot_general(k_ref[pl.ds(k0, kc), hs], qs_ref[:, hs], NT_DIMS,
                                 preferred_element_type=F32).astype(BF16) + bias
            lg_ref[h] = lg
            new_ms.append(jnp.maximum(ms[h], _rows_reduce(lg, jnp.max)))
        for h in range(heads):
            hs = slice(h * AT_HDIM, (h + 1) * AT_HDIM)
            alpha = jnp.exp2(ms[h].astype(F32) - new_ms[h].astype(F32))
            p_ref[h] = jnp.exp2(lg_ref[h] - new_ms[h])
            v_ext = jnp.concatenate([vt_ref[ci, hs, :], ones_rows], axis=0)
            acc_ref[h] = alpha * acc_ref[h] + jnp.dot(v_ext, p_ref[h], preferred_element_type=F32)
        return tuple(new_ms)

    init = tuple(jnp.full((1, tq), NEG_BIG, BF16) for _ in range(heads))
    lax.fori_loop(0, nchunks, chunk, init)

    live = (q0 + lax.broadcasted_iota(jnp.int32, (1, tq), 1)) >= META0
    for h in range(heads):
        num = acc_ref[h, 0:AT_HDIM, :]
        den = acc_ref[h, AT_HDIM:AT_HDIM + 1, :]
        out_t = jnp.where(live, num / den, 0.0)
        o_ref[:, h * AT_HDIM:(h + 1) * AT_HDIM] = out_t.T.astype(o_ref.dtype)


def _dsa_attn(z3, vt, bias):
    bsz, tp, _ = z3.shape
    gw = AT_HEADS * AT_HDIM // ATT_GROUPS
    aq_col = HG_HEADS * HG_DIM * 4 // gw
    ak_col = aq_col + ATT_GROUPS
    hpg = AT_HEADS // ATT_GROUPS
    return pl.pallas_call(
        _dsa_attn_kernel,
        grid=(bsz, ATT_GROUPS, tp // DSA_TQ),
        in_specs=[pl.BlockSpec((None, DSA_TQ, gw), lambda b, g, i: (b, i, aq_col + g)),
                  pl.BlockSpec((None, tp, gw), lambda b, g, i: (b, 0, ak_col + g), pipeline_mode=pl.Buffered(1)),
                  pl.BlockSpec((None, tp // DSA_KC, gw, DSA_KC), lambda b, g, i: (b, 0, g, 0),
                               pipeline_mode=pl.Buffered(1)),
                  pl.BlockSpec((None, None, tp, DSA_TQ), lambda b, g, i: (b, i, 0, 0))],
        out_specs=pl.BlockSpec((None, DSA_TQ, gw), lambda b, g, i: (b, i, g)),
        out_shape=jax.ShapeDtypeStruct((bsz, tp, AT_HEADS * AT_HDIM), BF16),
        scratch_shapes=[pltpu.VMEM((DSA_TQ, gw), BF16),
                        pltpu.VMEM((hpg, ACC_ROWS, DSA_TQ), F32),
                        pltpu.VMEM((hpg, DSA_KC, DSA_TQ), BF16),
                        pltpu.VMEM((hpg, DSA_KC, DSA_TQ), BF16)],
        compiler_params=_cparams(("parallel", "parallel", "arbitrary")),
        name="dsa_attn",
    )(z3, z3, vt, bias)


def _merge_kernel(yh_ref, ya_ref, ga_ref, gb_ref, wh_ref, wa_ref, o_ref):
    mh = jnp.dot(yh_ref[...], wh_ref[...], preferred_element_type=F32)
    ma = jnp.dot(ya_ref[...], wa_ref[...], preferred_element_type=F32)
    o = jax.nn.sigmoid(ga_ref[...].astype(F32)) * mh + jax.nn.sigmoid(gb_ref[...].astype(F32)) * ma
    o_ref[...] = o.astype(o_ref.dtype)


def _merge(yh, ya, z, wh, wa, tm, tn):
    m, kdim = yh.shape
    n = wh.shape[1]
    ga_col = (z.shape[1] - 2 * n) // tn
    gb_col = (z.shape[1] - n) // tn
    return pl.pallas_call(
        _merge_kernel,
        grid=(m // tm, n // tn),
        in_specs=[pl.BlockSpec((tm, kdim), lambda i, j: (i, 0)),
                  pl.BlockSpec((tm, kdim), lambda i, j: (i, 0)),
                  pl.BlockSpec((tm, tn), lambda i, j: (i, ga_col + j)),
                  pl.BlockSpec((tm, tn), lambda i, j: (i, gb_col + j)),
                  pl.BlockSpec((kdim, tn), lambda i, j: (0, j)),
                  pl.BlockSpec((kdim, tn), lambda i, j: (0, j))],
        out_specs=pl.BlockSpec((tm, tn), lambda i, j: (i, j)),
        out_shape=jax.ShapeDtypeStruct((m, n), BF16),
        compiler_params=_cparams(("parallel", "arbitrary")),
        name="branch_merge",
    )(yh, ya, z, z, wh, wa)


def _out_proj_kernel(x_ref, pre_ref, m_ref, w_ref, nw_ref, h1_ref, u2_ref, *, bps):
    is_prefix = pl.program_id(0) % bps == 0
    h = jnp.where(is_prefix, pre_ref[...], x_ref[...])
    h1 = h + jnp.dot(m_ref[...], w_ref[...], preferred_element_type=F32)
    h1_ref[...] = h1
    u2_ref[...] = _rms_rows(h1, nw_ref[...]).astype(u2_ref.dtype)


def _out_proj(x2, prefix, mm, w, nw, bsz, tp):
    m, d = mm.shape
    tm = SEQ_OFF
    bps = tp // tm
    row = lambda r: (r, 0)
    fixed = lambda r: (0, 0)
    return pl.pallas_call(
        functools.partial(_out_proj_kernel, bps=bps),
        grid=(m // tm,),
        in_specs=[pl.BlockSpec((tm, d), lambda r: (_x_block(r, bps), 0)), pl.BlockSpec((tm, d), fixed),
                  pl.BlockSpec((tm, d), row), pl.BlockSpec((d, d), fixed), pl.BlockSpec((1, d), fixed)],
        out_specs=[pl.BlockSpec((tm, d), row), pl.BlockSpec((tm, d), row)],
        out_shape=[jax.ShapeDtypeStruct((m, d), F32), jax.ShapeDtypeStruct((m, d), BF16)],
        compiler_params=_cparams(("parallel",)),
        name="out_proj",
    )(x2, prefix, mm, w, nw)


def _ffn_up_kernel(u_ref, wg_ref, wu_ref, cw_ref, cb_ref, o_ref, tail_ref, wg_scr, wu_scr):
    tm = u_ref.shape[0]

    @pl.when(pl.program_id(1) == 0)
    def _():
        tail_ref[...] = jnp.zeros_like(tail_ref)
        wg_scr[...] = wg_ref[...].astype(BF16)
        wu_scr[...] = wu_ref[...].astype(BF16)

    u = u_ref[...]
    g0 = jnp.dot(u, wg_scr[...], preferred_element_type=F32)
    up = jnp.dot(u, wu_scr[...], preferred_element_type=F32)
    row = lax.broadcasted_iota(jnp.int32, (tm, 1), 0)
    prev1 = tail_ref[7:8, :]
    prev2 = tail_ref[6:7, :]
    g1 = jnp.where(row == 0, prev1, pltpu.roll(g0, 1, axis=0))
    g2 = jnp.where(row == 0, prev2, jnp.where(row == 1, prev1, pltpu.roll(g0, 2, axis=0)))
    tail_ref[...] = g0[tm - 8:tm, :]
    a = cw_ref[0:1, :] * g2 + cw_ref[1:2, :] * g1 + cw_ref[2:3, :] * g0 + cb_ref[...]
    o_ref[...] = (a * jax.nn.sigmoid(a) * up).astype(o_ref.dtype)


def _ffn_up(u2, wg, wu, cw, cb, tm, tn):
    m, d = u2.shape
    n = wg.shape[1]
    return pl.pallas_call(
        _ffn_up_kernel,
        grid=(n // tn, m // tm),
        in_specs=[pl.BlockSpec((tm, d), lambda j, i: (i, 0)),
                  pl.BlockSpec((d, tn), lambda j, i: (0, j)),
                  pl.BlockSpec((d, tn), lambda j, i: (0, j)),
                  pl.BlockSpec((8, tn), lambda j, i: (0, j)),
                  pl.BlockSpec((1, tn), lambda j, i: (0, j))],
        out_specs=pl.BlockSpec((tm, tn), lambda j, i: (i, j)),
        out_shape=jax.ShapeDtypeStruct((m, n), BF16),
        scratch_shapes=[pltpu.VMEM((8, tn), F32), pltpu.VMEM((d, tn), BF16), pltpu.VMEM((d, tn), BF16)],
        compiler_params=_cparams(("parallel", "arbitrary")),
        name="ffn_up",
    )(u2, wg, wu, cw, cb)


def _ffn_down_kernel(g_ref, wd_ref, h1_ref, nw_ref, o_ref):
    h2 = h1_ref[...] + jnp.dot(g_ref[...], wd_ref[...], preferred_element_type=F32)
    o_ref[...] = _rms_rows(h2, nw_ref[...])


def _ffn_down(gact, wd, h1, nw, bsz, seq, tp, tm):
    m, kdim = gact.shape
    d = wd.shape[1]
    per_seq = seq // tm
    off = SEQ_OFF // tm

    def row_in(i):
        return ((i // per_seq) * (tp // tm) + off + i % per_seq, 0)

    return pl.pallas_call(
        _ffn_down_kernel,
        grid=(bsz * per_seq,),
        in_specs=[pl.BlockSpec((tm, kdim), row_in),
                  pl.BlockSpec((kdim, d), lambda i: (0, 0), pipeline_mode=pl.Buffered(1)),
                  pl.BlockSpec((tm, d), row_in),
                  pl.BlockSpec((1, d), lambda i: (0, 0))],
        out_specs=pl.BlockSpec((tm, d), lambda i: (i, 0)),
        out_shape=jax.ShapeDtypeStruct((bsz * seq, d), F32),
        compiler_params=_cparams(("parallel",)),
        name="ffn_down",
    )(gact, wd, h1, nw)


def kernel(x, meta_tokens, attn_norm_w, w_in, hgrn_lb_logits, hgrn_norm_w, idx_k_norm_w, w_branch_hgrn,
           w_branch_dsa, w_out, ffn_norm_w, w_ffn_gate, w_ffn_up, ffn_conv_w, ffn_conv_b, w_ffn_down,
           final_norm_w):
    bsz, seq, d = x.shape
    depth = w_in.shape[0]
    topk = min(IDX_TOPK_MAX, seq // 4)
    tp = SEQ_OFF + seq
    m = bsz * tp
    hg_w = HG_HEADS * HG_DIM
    at_w = AT_HEADS * AT_HDIM
    n_main_lo = 4 * hg_w + 3 * at_w + IDX_HEADS * IDX_HDIM
    n_small = IDX_HDIM + IDX_HEADS

    prefix = jnp.concatenate([jnp.zeros((META0, d), x.dtype), meta_tokens.astype(x.dtype)], axis=0)
    x2 = x.reshape(bsz * seq, d)
    lbs = jnp.cumsum(jax.nn.softmax(hgrn_lb_logits.astype(F32), axis=0), axis=0)

    assert depth == 1, "multi-layer stacks are not supported"
    out = None
    for l in range(depth):
        wt = jnp.swapaxes(w_in[l], 0, 1)
        w_small = jnp.pad(wt[n_main_lo:n_main_lo + n_small], ((0, LANES - n_small), (0, 0))).T.astype(BF16)
        knw = jnp.pad(idx_k_norm_w[l], (0, LANES - IDX_HDIM)).reshape(1, LANES)

        u, zs, kt, kb = _embed(x2, prefix, attn_norm_w[l].reshape(1, d), w_small, knw, bsz, tp)
        z = _in_proj(u, wt, n_main_lo, n_main_lo + n_small, 2 * d, tm=MM_ROWS, tn=MM_COLS)
        z3 = z.reshape(bsz, tp, z.shape[1])

        y_h = _hgrn(z3, lbs[l].reshape(1, hg_w), hgrn_norm_w[l].reshape(1, HG_DIM))

        wt = zs.reshape(bsz, tp, LANES)[:, :, IDX_HDIM:IDX_HDIM + IDX_HEADS]
        wt = jnp.swapaxes(wt * (IDX_HEADS ** -0.5 * IDX_HDIM ** -0.5), 1, 2)
        bias = _dsa_index(z3, wt, kt.reshape(bsz, tp, LANES), kb.reshape(bsz, tp, LANES), topk)
        av = z3[:, :, 4 * hg_w + 2 * at_w:4 * hg_w + 3 * at_w]
        vt = jnp.swapaxes(av.reshape(bsz, tp // DSA_KC, DSA_KC, at_w), 2, 3)
        y_a = _dsa_attn(z3, vt, bias)

        mm = _merge(y_h.reshape(m, hg_w), y_a.reshape(m, at_w), z,
                    w_branch_hgrn[l].astype(BF16), w_branch_dsa[l].astype(BF16), tm=MM_ROWS, tn=MM_COLS)
        h1, u2 = _out_proj(x2, prefix, mm, w_out[l].astype(BF16), ffn_norm_w[l].reshape(1, d), bsz, tp)

        cw = jnp.pad(ffn_conv_w[l], ((0, 8 - CONV_W), (0, 0)))
        gact = _ffn_up(u2, w_ffn_gate[l], w_ffn_up[l], cw,
                       ffn_conv_b[l].reshape(1, -1), tm=MM_ROWS, tn=FFN_UP_COLS)
        out = _ffn_down(gact, w_ffn_down[l].astype(BF16), h1, final_norm_w.reshape(1, d),
                        bsz, seq, tp, tm=FFN_DOWN_ROWS)
    return out.reshape(bsz, seq, d)
```

```python
import functools
import math

import jax
import jax.numpy as jnp
import numpy as np
from jax import lax
from jax.experimental import pallas as pl
from jax.experimental.pallas import tpu as pltpu

N_META = 16
EPS = 1e-6
HG_HEADS = 8
HG_DIM = 128
AT_HEADS = 8
AT_HDIM = 128
IDX_HEADS = 16
IDX_HDIM = 64
IDX_TOPK_MAX = 256
CONV_W = 3

LANES = 128
SUBLANES = 8
SEQ_OFF = 512
META0 = SEQ_OFF - N_META
NEG_BIG = -1e30
INT_MIN = -(2 ** 31)

HG_CHUNK = 128
HG_DIAG = 4
SCAN_STEPS = (1, 2, 4)
HG_STEP = 512
HG_HPS = 4
DSA_TQ = 256
DSA_KC = 512
COUNT_ROWS = 64
HALF_OFF = 2 ** 15
ATT_GROUPS = 1
REDUCE_ROWS = 64
ACC_ROWS = AT_HDIM + 16
MM_ROWS = 1024
MM_COLS = 1024
FFN_UP_COLS = 512
FFN_DOWN_ROWS = 256
VMEM_LIMIT = 56 * 1024 * 1024

F32 = jnp.float32
BF16 = jnp.bfloat16
NT_DIMS = (((1,), (1,)), ((), ()))
TN_DIMS = (((0,), (0,)), ((), ()))


def _cparams(sem):
    return pltpu.CompilerParams(dimension_semantics=sem, vmem_limit_bytes=VMEM_LIMIT)


def _rms_rows(x, w):
    ms = jnp.mean(x * x, axis=-1, keepdims=True)
    return x * lax.rsqrt(ms + EPS) * w


def _rows_reduce(x, reduce):
    rows, cols = x.shape
    if rows > REDUCE_ROWS:
        x = reduce(x.reshape(rows // REDUCE_ROWS, REDUCE_ROWS, cols), axis=0)
    return reduce(x, axis=0, keepdims=True)


def _x_block(r, bps):
    return (r // bps) * (bps - 1) + jnp.maximum(r % bps - 1, 0)


def _embed_kernel(x_ref, pre_ref, nw_ref, w_ref, knw_ref, u_ref, zs_ref, kt_ref, kb_ref, *, bps):
    is_prefix = pl.program_id(0) % bps == 0
    h = jnp.where(is_prefix, pre_ref[...], x_ref[...])
    u = _rms_rows(h, nw_ref[...]).astype(BF16)
    u_ref[...] = u
    zs = jnp.dot(u, w_ref[...], preferred_element_type=F32)
    lane = lax.broadcasted_iota(jnp.int32, zs.shape, 1)
    ik = jnp.where(lane < IDX_HDIM, zs, 0.0)
    ms = jnp.sum(ik * ik, axis=-1, keepdims=True) / IDX_HDIM
    kn = ik * lax.rsqrt(ms + EPS) * knw_ref[...]
    zs_ref[...] = zs
    kt_ref[...] = kn.astype(BF16)
    kb_ref[...] = pltpu.roll(kn, IDX_HDIM, axis=1).astype(BF16)


def _embed(x2, prefix, nw, w, knw, bsz, tp):
    d = x2.shape[1]
    tm = SEQ_OFF
    bps = tp // tm
    m = bsz * tp
    row = lambda r: (r, 0)
    fixed = lambda r: (0, 0)
    return pl.pallas_call(
        functools.partial(_embed_kernel, bps=bps),
        grid=(m // tm,),
        in_specs=[pl.BlockSpec((tm, d), lambda r: (_x_block(r, bps), 0)),
                  pl.BlockSpec((tm, d), fixed), pl.BlockSpec((1, d), fixed),
                  pl.BlockSpec((d, LANES), fixed), pl.BlockSpec((1, LANES), fixed)],
        out_specs=[pl.BlockSpec((tm, d), row)] + [pl.BlockSpec((tm, LANES), row)] * 3,
        out_shape=[jax.ShapeDtypeStruct((m, d), BF16),
                   jax.ShapeDtypeStruct((m, LANES), F32),
                   jax.ShapeDtypeStruct((m, LANES), BF16),
                   jax.ShapeDtypeStruct((m, LANES), BF16)],
        compiler_params=_cparams(("parallel",)),
        name="embed_norm",
    )(x2, prefix, nw, w, knw)


def _in_proj_kernel(u_ref, wt_ref, o_ref, w_scr):
    @pl.when(pl.program_id(1) == 0)
    def _():
        w_scr[...] = wt_ref[...].astype(BF16)

    o_ref[...] = lax.dot_general(u_ref[...], w_scr[...], NT_DIMS,
                                 preferred_element_type=F32).astype(o_ref.dtype)


def _in_proj(u, wt, n_lo_cols, hi_start, n_hi_cols, tm, tn):
    m, d = u.shape
    n_lo = n_lo_cols // tn
    n_blocks = n_lo + n_hi_cols // tn

    assert hi_start % SUBLANES == 0

    def w_rows(j, i):
        return (pl.multiple_of(jnp.where(j < n_lo, j * tn, hi_start + (j - n_lo) * tn), SUBLANES), 0)

    return pl.pallas_call(
        _in_proj_kernel,
        grid=(n_blocks, m // tm),
        in_specs=[pl.BlockSpec((tm, d), lambda j, i: (i, 0)),
                  pl.BlockSpec((pl.Element(tn), pl.Element(d)), w_rows)],
        out_specs=pl.BlockSpec((tm, tn), lambda j, i: (i, j)),
        out_shape=jax.ShapeDtypeStruct((m, n_blocks * tn), BF16),
        scratch_shapes=[pltpu.VMEM((tn, d), BF16)],
        compiler_params=_cparams(("parallel", "arbitrary")),
        name="in_proj",
    )(u, wt)


def _hgrn_masks():
    c = HG_CHUNK
    t = np.arange(c)[:, None]
    s = np.arange(c)[None, :]
    scan = [np.broadcast_to((t % 8) >= k, (c, c)) for k in SCAN_STEPS]
    levels = [(t // (2 * m) == s // (2 * m)) & ((t // m) % 2 == 1) & ((s // m) % 2 == 0) for m in _hgrn_levels()]
    diags = [(s == t - d) & ((t % HG_DIAG) >= d) for d in range(HG_DIAG)]
    return np.stack(scan + levels + diags).astype(np.float32)


def _hgrn_levels():
    m, out = HG_CHUNK // 2, []
    while m >= HG_DIAG:
        out.append(m)
        m //= 2
    return out


def _sigmoid(x):
    return 1.0 / (1.0 + jnp.exp(-x))


def _hgrn_kernel(q_ref, f_ref, i_ref, g_ref, lb_ref, nw_ref, mk_ref, o_ref, st_ref):
    c = HG_CHUNK
    levels = _hgrn_levels()
    n_scan = len(SCAN_STEPS)

    @pl.when(pl.program_id(2) == 0)
    def _():
        st_ref[...] = jnp.zeros_like(st_ref)

    def chunk(ci, hh, st):
        rows = pl.ds(ci * c, c)
        cols = slice(hh * HG_DIM, (hh + 1) * HG_DIM)
        lb = lb_ref[:, cols]
        oml = 1.0 - lb
        qz = q_ref[rows, cols].astype(F32)
        fz = f_ref[rows, cols].astype(F32)
        v = i_ref[rows, cols]
        gz = g_ref[rows, cols].astype(F32)

        q = qz * _sigmoid(qz)
        sf = _sigmoid(fz)
        f = lb + oml * sf
        kc = oml * (1.0 - sf)

        bw = jnp.log2(f)
        for k, step in enumerate(SCAN_STEPS):
            bw = bw + pltpu.roll(bw, step, axis=0) * mk_ref[k]
        run = jnp.zeros((1, HG_DIM), F32)
        tiles = []
        for i in range(c // 8):
            tiles.append(bw[8 * i:8 * i + 8, :] + run)
            run = run + bw[8 * i + 7:8 * i + 8, :]
        b = jnp.concatenate(tiles, axis=0)

        a = jnp.zeros((c, c), F32)
        for k, m in enumerate(levels):
            piv = jnp.concatenate(
                [jnp.broadcast_to(b[blk * 2 * m + m - 1: blk * 2 * m + m, :], (2 * m, HG_DIM))
                 for blk in range(c // (2 * m))], axis=0)
            x = jnp.exp2(-jnp.abs(b - piv))
            am = lax.dot_general((q * x).astype(BF16), (kc * x).astype(BF16), NT_DIMS,
                                 preferred_element_type=F32)
            a = a + am * mk_ref[n_scan + k]

        gd = None
        for d in range(HG_DIAG):
            if d == 0:
                xd = q * kc
            else:
                fd = f if d == 1 else pltpu.roll(f, d - 1, axis=0)
                gd = fd if gd is None else gd * fd
                xd = q * pltpu.roll(kc, d, axis=0) * gd
            a = a + jnp.sum(xd, axis=-1, keepdims=True) * mk_ref[n_scan + len(levels) + d]

        qd = (q * jnp.exp2(b)).astype(BF16)
        o = lax.dot_general(qd, st.astype(BF16), NT_DIMS, preferred_element_type=F32)
        o = o + jnp.dot(a.astype(BF16), v, preferred_element_type=F32)

        b_last = b[c - 1:c, :]
        kd = (kc * jnp.exp2(b_last - b)).astype(BF16)
        st = st * jnp.exp2(b_last) + lax.dot_general(v, kd, TN_DIMS, preferred_element_type=F32)

        y = _rms_rows(o, nw_ref[...]) * (gz * _sigmoid(gz))
        o_ref[rows, cols] = y.astype(o_ref.dtype)
        return st

    sts = [st_ref[hh] for hh in range(HG_HPS)]
    for ci in range(q_ref.shape[0] // c):
        for hh in range(HG_HPS):
            sts[hh] = chunk(ci, hh, sts[hh])
    for hh in range(HG_HPS):
        st_ref[hh] = sts[hh]


def _hgrn(z3, lb, nw):
    bsz, tp, _ = z3.shape
    masks = jnp.asarray(_hgrn_masks())

    hw = HG_HPS * HG_DIM
    hgroups = HG_HEADS // HG_HPS

    def zspec(group):
        return pl.BlockSpec((None, HG_STEP, hw), lambda b, h, s: (b, s, group * hgroups + h))

    return pl.pallas_call(
        _hgrn_kernel,
        grid=(bsz, hgroups, tp // HG_STEP),
        in_specs=[zspec(0), zspec(1), zspec(2), zspec(3),
                  pl.BlockSpec((1, hw), lambda b, h, s: (0, h)),
                  pl.BlockSpec((1, HG_DIM), lambda b, h, s: (0, 0)),
                  pl.BlockSpec(masks.shape, lambda b, h, s: (0, 0, 0))],
        out_specs=pl.BlockSpec((None, HG_STEP, hw), lambda b, h, s: (b, s, h)),
        out_shape=jax.ShapeDtypeStruct((bsz, tp, HG_HEADS * HG_DIM), BF16),
        scratch_shapes=[pltpu.VMEM((HG_HPS, HG_DIM, HG_DIM), F32)],
        compiler_params=_cparams(("parallel", "parallel", "arbitrary")),
        name="hgrn2",
    )(z3, z3, z3, z3, lb, nw, masks)


def _causal_chunks(q0, tq):
    return jnp.maximum(q0 + tq - 1 - META0, 0) // DSA_KC + 1


def _real_keys(a):
    return jnp.roll(a, -META0, axis=1)


def _dsa_index_kernel(iq_ref, wt_ref, kt_ref, kb_ref, bias_ref, keys_ref, hi_ref, lo_ref, acc_ref, *, topk):
    tq = iq_ref.shape[0]
    kc = DSA_KC
    tp = kt_ref.shape[0]
    q0 = pl.program_id(1) * tq
    nchunks = _causal_chunks(q0, tq)
    qpos = q0 - META0 + lax.broadcasted_iota(jnp.int32, (1, tq), 1)
    int_min = jnp.int32(INT_MIN)
    half = jnp.int32(HALF_OFF)

    def score_chunk(ci, carry):
        k0 = pl.multiple_of(ci * kc, kc)
        kt = kt_ref[pl.ds(k0, kc), :]
        kb = kb_ref[pl.ds(k0, kc), :]
        for p in range(IDX_HEADS // 2):
            qp = iq_ref[:, p * LANES:(p + 1) * LANES]
            st = lax.dot_general(kt, qp, NT_DIMS, preferred_element_type=F32)
            sb = lax.dot_general(kb, qp, NT_DIMS, preferred_element_type=F32)
            part = (wt_ref[2 * p:2 * p + 1, :] * jnp.maximum(st, 0.0)
                    + wt_ref[2 * p + 1:2 * p + 2, :] * jnp.maximum(sb, 0.0))
            if p == 0:
                acc_ref[...] = part
            else:
                acc_ref[...] += part
        score = acc_ref[...] + 0.0
        bits = lax.bitcast_convert_type(score, jnp.int32)
        key = bits ^ ((bits >> 31) & jnp.int32(0x7FFFFFFF))
        kpos = k0 + lax.broadcasted_iota(jnp.int32, (kc, 1), 0)
        valid = kpos <= qpos
        key = jnp.where(valid, key, int_min)
        keys_ref[pl.ds(k0, kc), :] = key
        hi_ref[pl.ds(k0, kc), :] = (key >> 16).astype(jnp.int16)
        lo_ref[pl.ds(k0, kc), :] = (((key << 16) ^ int_min) >> 16).astype(jnp.int16)
        return carry

    lax.fori_loop(0, nchunks, score_chunk, 0)

    def count_ge(src_ref, cand_u):
        cand = (cand_u - half).astype(jnp.int16)

        def count_chunk(ci, cnt):
            k0 = pl.multiple_of(ci * kc, kc)
            for r in range(kc // COUNT_ROWS):
                ks = src_ref[pl.ds(k0 + r * COUNT_ROWS, COUNT_ROWS), :]
                cnt = cnt + jnp.where(ks >= cand, jnp.int16(1), jnp.int16(0))
            return cnt

        cnt = lax.fori_loop(0, nchunks, count_chunk, jnp.zeros((COUNT_ROWS, tq), jnp.int16))
        return jnp.sum(cnt.astype(F32), axis=0, keepdims=True)

    def bisect16(src_ref, need, count_at_zero):
        def bit_pass(i, carry):
            u, at_u, above_u = carry
            cand_u = u | lax.shift_left(jnp.int32(1), 15 - i)
            c = count_ge(src_ref, cand_u)
            ok = c >= need
            return jnp.where(ok, cand_u, u), jnp.where(ok, c, at_u), jnp.where(ok, above_u, c)
        init = (jnp.zeros((1, tq), jnp.int32), count_at_zero, jnp.zeros((1, tq), F32))
        return lax.fori_loop(0, 16, bit_pass, init)

    kf = jnp.full((1, tq), float(topk), F32)
    u_hi, n_hi, above = bisect16(hi_ref, kf, kf)
    hi_thr = (u_hi - half).astype(jnp.int16)

    def narrow_chunk(ci, carry):
        rows = pl.ds(pl.multiple_of(ci * kc, kc), kc)
        lo_ref[rows, :] = jnp.where(hi_ref[rows, :] == hi_thr, lo_ref[rows, :], jnp.int16(-HALF_OFF))
        return carry

    lax.fori_loop(0, nchunks, narrow_chunk, 0)
    u_lo, n_lo, _ = bisect16(lo_ref, topk - above, n_hi - above)
    few = u_hi == 0
    thr = jnp.where(few, int_min + 1, ((u_hi - half) << 16) | u_lo)
    tied = jnp.where(few, 0.0, above + n_lo - topk)
    any_tied = jnp.max(tied) > 0.0

    @pl.when(jnp.logical_not(any_tied))
    def _():
        def write_chunk(ci, carry):
            k0 = pl.multiple_of(ci * kc, kc)
            ks = keys_ref[pl.ds(k0, kc), :]
            bias_ref[pl.ds(k0, kc), :] = jnp.where(ks >= thr, 0.0, NEG_BIG).astype(bias_ref.dtype)
            return carry

        lax.fori_loop(0, nchunks, write_chunk, 0)

    @pl.when(any_tied)
    def _():
        def count_gt(ci, cnt):
            k0 = pl.multiple_of(ci * kc, kc)
            return cnt + jnp.sum(jnp.where(keys_ref[pl.ds(k0, kc), :] > thr, 1.0, 0.0), axis=0, keepdims=True)

        room = topk - lax.fori_loop(0, nchunks, count_gt, jnp.zeros((1, tq), F32))
        lower_tri = (lax.broadcasted_iota(jnp.int32, (kc, kc), 0)
                     >= lax.broadcasted_iota(jnp.int32, (kc, kc), 1)).astype(BF16)

        def write_chunk(ci, seen):
            k0 = pl.multiple_of(ci * kc, kc)
            ks = keys_ref[pl.ds(k0, kc), :]
            eq = ks == thr
            rank = seen + jnp.dot(lower_tri, jnp.where(eq, 1.0, 0.0).astype(BF16), preferred_element_type=F32)
            sel = (ks > thr) | (eq & (rank <= room))
            bias_ref[pl.ds(k0, kc), :] = jnp.where(sel, 0.0, NEG_BIG).astype(bias_ref.dtype)
            return rank[kc - 1:kc, :]

        lax.fori_loop(0, nchunks, write_chunk, jnp.zeros((1, tq), F32))

    def fill_chunk(ci, carry):
        k0 = pl.multiple_of(ci * kc, kc)
        bias_ref[pl.ds(k0, kc), :] = jnp.full((kc, tq), NEG_BIG, bias_ref.dtype)
        return carry

    lax.fori_loop(nchunks, tp // kc, fill_chunk, 0)


def _dsa_index(z3, wt, kt, kb, topk):
    bsz, tp, _ = z3.shape
    iq_col = (HG_HEADS * HG_DIM * 4 + AT_HEADS * AT_HDIM * 3) // (IDX_HEADS * IDX_HDIM)
    return pl.pallas_call(
        functools.partial(_dsa_index_kernel, topk=topk),
        grid=(bsz, tp // DSA_TQ),
        in_specs=[pl.BlockSpec((None, DSA_TQ, IDX_HEADS * IDX_HDIM), lambda b, i: (b, i, iq_col)),
                  pl.BlockSpec((None, IDX_HEADS, DSA_TQ), lambda b, i: (b, 0, i)),
                  pl.BlockSpec((None, tp, LANES), lambda b, i: (b, 0, 0)),
                  pl.BlockSpec((None, tp, LANES), lambda b, i: (b, 0, 0))],
        out_specs=pl.BlockSpec((None, None, tp, DSA_TQ), lambda b, i: (b, i, 0, 0)),
        out_shape=jax.ShapeDtypeStruct((bsz, tp // DSA_TQ, tp, DSA_TQ), BF16),
        scratch_shapes=[pltpu.VMEM((tp, DSA_TQ), jnp.int32), pltpu.VMEM((tp, DSA_TQ), jnp.int16),
                        pltpu.VMEM((tp, DSA_TQ), jnp.int16), pltpu.VMEM((DSA_KC, DSA_TQ), F32)],
        compiler_params=_cparams(("parallel", "arbitrary")),
        name="dsa_index",
    )(z3, wt, kt, kb)


def _dsa_attn_kernel(aq_ref, k_ref, vt_ref, bias_ref, o_ref, qs_ref, acc_ref, lg_ref, p_ref):
    tq = aq_ref.shape[0]
    kc = DSA_KC
    heads = aq_ref.shape[1] // AT_HDIM
    q0 = pl.program_id(2) * tq
    nchunks = _causal_chunks(q0, tq)

    qs_ref[...] = (aq_ref[...].astype(F32) * (AT_HDIM ** -0.5 * math.log2(math.e))).astype(qs_ref.dtype)
    acc_ref[...] = jnp.zeros(acc_ref.shape, F32)
    ones_rows = (lax.broadcasted_iota(jnp.int32, (ACC_ROWS - AT_HDIM, kc), 0) == 0).astype(BF16)

    def chunk(ci, ms):
        k0 = pl.multiple_of(ci * kc, kc)
        bias = bias_ref[pl.ds(k0, kc), :]
        new_ms = []
        for h in range(heads):
            hs = slice(h * AT_HDIM, (h + 1) * AT_HDIM)
            lg = lax.dot_general(k_ref[pl.ds(k0, kc), hs], qs_ref[:, hs], NT_DIMS,
                                 preferred_element_type=F32).astype(BF16) + bias
            lg_ref[h] = lg
            new_ms.append(jnp.maximum(ms[h], _rows_reduce(lg, jnp.max)))
        for h in range(heads):
            hs = slice(h * AT_HDIM, (h + 1) * AT_HDIM)
            alpha = jnp.exp2(ms[h].astype(F32) - new_ms[h].astype(F32))
            p_ref[h] = jnp.exp2(lg_ref[h] - new_ms[h])
            v_ext = jnp.concatenate([vt_ref[ci, hs, :], ones_rows], axis=0)
            acc_ref[h] = alpha * acc_ref[h] + jnp.dot(v_ext, p_ref[h], preferred_element_type=F32)
        return tuple(new_ms)

    init = tuple(jnp.full((1, tq), NEG_BIG, BF16) for _ in range(heads))
    lax.fori_loop(0, nchunks, chunk, init)

    live = (q0 + lax.broadcasted_iota(jnp.int32, (1, tq), 1)) >= META0
    for h in range(heads):
        num = acc_ref[h, 0:AT_HDIM, :]
        den = acc_ref[h, AT_HDIM:AT_HDIM + 1, :]
        out_t = jnp.where(live, num / den, 0.0)
        o_ref[:, h * AT_HDIM:(h + 1) * AT_HDIM] = out_t.T.astype(o_ref.dtype)


def _dsa_attn(z3, kk, vt, bias):
    bsz, tp, _ = z3.shape
    gw = AT_HEADS * AT_HDIM // ATT_GROUPS
    aq_col = HG_HEADS * HG_DIM * 4 // gw
    hpg = AT_HEADS // ATT_GROUPS
    return pl.pallas_call(
        _dsa_attn_kernel,
        grid=(bsz, ATT_GROUPS, tp // DSA_TQ),
        in_specs=[pl.BlockSpec((None, DSA_TQ, gw), lambda b, g, i: (b, i, aq_col + g)),
                  pl.BlockSpec((None, tp, gw), lambda b, g, i: (b, 0, g), pipeline_mode=pl.Buffered(1)),
                  pl.BlockSpec((None, tp // DSA_KC, gw, DSA_KC), lambda b, g, i: (b, 0, g, 0),
                               pipeline_mode=pl.Buffered(1)),
                  pl.BlockSpec((None, None, tp, DSA_TQ), lambda b, g, i: (b, i, 0, 0))],
        out_specs=pl.BlockSpec((None, DSA_TQ, gw), lambda b, g, i: (b, i, g)),
        out_shape=jax.ShapeDtypeStruct((bsz, tp, AT_HEADS * AT_HDIM), BF16),
        scratch_shapes=[pltpu.VMEM((DSA_TQ, gw), BF16),
                        pltpu.VMEM((hpg, ACC_ROWS, DSA_TQ), F32),
                        pltpu.VMEM((hpg, DSA_KC, DSA_TQ), BF16),
                        pltpu.VMEM((hpg, DSA_KC, DSA_TQ), BF16)],
        compiler_params=_cparams(("parallel", "parallel", "arbitrary")),
        name="dsa_attn",
    )(z3, kk, vt, bias)


def _merge_kernel(yh_ref, ya_ref, ga_ref, gb_ref, wh_ref, wa_ref, o_ref):
    mh = jnp.dot(yh_ref[...], wh_ref[...], preferred_element_type=F32)
    ma = jnp.dot(ya_ref[...], wa_ref[...], preferred_element_type=F32)
    o = jax.nn.sigmoid(ga_ref[...].astype(F32)) * mh + jax.nn.sigmoid(gb_ref[...].astype(F32)) * ma
    o_ref[...] = o.astype(o_ref.dtype)


def _merge(yh, ya, z, wh, wa, tm, tn):
    m, kdim = yh.shape
    n = wh.shape[1]
    ga_col = (z.shape[1] - 2 * n) // tn
    gb_col = (z.shape[1] - n) // tn
    return pl.pallas_call(
        _merge_kernel,
        grid=(m // tm, n // tn),
        in_specs=[pl.BlockSpec((tm, kdim), lambda i, j: (i, 0)),
                  pl.BlockSpec((tm, kdim), lambda i, j: (i, 0)),
                  pl.BlockSpec((tm, tn), lambda i, j: (i, ga_col + j)),
                  pl.BlockSpec((tm, tn), lambda i, j: (i, gb_col + j)),
                  pl.BlockSpec((kdim, tn), lambda i, j: (0, j)),
                  pl.BlockSpec((kdim, tn), lambda i, j: (0, j))],
        out_specs=pl.BlockSpec((tm, tn), lambda i, j: (i, j)),
        out_shape=jax.ShapeDtypeStruct((m, n), BF16),
        compiler_params=_cparams(("parallel", "arbitrary")),
        name="branch_merge",
    )(yh, ya, z, z, wh, wa)


def _out_proj_kernel(x_ref, pre_ref, m_ref, w_ref, nw_ref, h1_ref, u2_ref, *, bps):
    is_prefix = pl.program_id(0) % bps == 0
    h = jnp.where(is_prefix, pre_ref[...], x_ref[...])
    h1 = h + jnp.dot(m_ref[...], w_ref[...], preferred_element_type=F32)
    h1_ref[...] = h1
    u2_ref[...] = _rms_rows(h1, nw_ref[...]).astype(u2_ref.dtype)


def _out_proj(x2, prefix, mm, w, nw, bsz, tp):
    m, d = mm.shape
    tm = SEQ_OFF
    bps = tp // tm
    row = lambda r: (r, 0)
    fixed = lambda r: (0, 0)
    return pl.pallas_call(
        functools.partial(_out_proj_kernel, bps=bps),
        grid=(m // tm,),
        in_specs=[pl.BlockSpec((tm, d), lambda r: (_x_block(r, bps), 0)), pl.BlockSpec((tm, d), fixed),
                  pl.BlockSpec((tm, d), row), pl.BlockSpec((d, d), fixed), pl.BlockSpec((1, d), fixed)],
        out_specs=[pl.BlockSpec((tm, d), row), pl.BlockSpec((tm, d), row)],
        out_shape=[jax.ShapeDtypeStruct((m, d), F32), jax.ShapeDtypeStruct((m, d), BF16)],
        compiler_params=_cparams(("parallel",)),
        name="out_proj",
    )(x2, prefix, mm, w, nw)


def _ffn_up_kernel(u_ref, wg_ref, wu_ref, cw_ref, cb_ref, o_ref, tail_ref, wg_scr, wu_scr):
    tm = u_ref.shape[0]

    @pl.when(pl.program_id(1) == 0)
    def _():
        tail_ref[...] = jnp.zeros_like(tail_ref)
        wg_scr[...] = wg_ref[...].astype(BF16)
        wu_scr[...] = wu_ref[...].astype(BF16)

    u = u_ref[...]
    g0 = jnp.dot(u, wg_scr[...], preferred_element_type=F32)
    up = jnp.dot(u, wu_scr[...], preferred_element_type=F32)
    row = lax.broadcasted_iota(jnp.int32, (tm, 1), 0)
    prev1 = tail_ref[7:8, :]
    prev2 = tail_ref[6:7, :]
    g1 = jnp.where(row == 0, prev1, pltpu.roll(g0, 1, axis=0))
    g2 = jnp.where(row == 0, prev2, jnp.where(row == 1, prev1, pltpu.roll(g0, 2, axis=0)))
    tail_ref[...] = g0[tm - 8:tm, :]
    a = cw_ref[0:1, :] * g2 + cw_ref[1:2, :] * g1 + cw_ref[2:3, :] * g0 + cb_ref[...]
    o_ref[...] = (a * jax.nn.sigmoid(a) * up).astype(o_ref.dtype)


def _ffn_up(u2, wg, wu, cw, cb, tm, tn):
    m, d = u2.shape
    n = wg.shape[1]
    return pl.pallas_call(
        _ffn_up_kernel,
        grid=(n // tn, m // tm),
        in_specs=[pl.BlockSpec((tm, d), lambda j, i: (i, 0)),
                  pl.BlockSpec((d, tn), lambda j, i: (0, j)),
                  pl.BlockSpec((d, tn), lambda j, i: (0, j)),
                  pl.BlockSpec((8, tn), lambda j, i: (0, j)),
                  pl.BlockSpec((1, tn), lambda j, i: (0, j))],
        out_specs=pl.BlockSpec((tm, tn), lambda j, i: (i, j)),
        out_shape=jax.ShapeDtypeStruct((m, n), BF16),
        scratch_shapes=[pltpu.VMEM((8, tn), F32), pltpu.VMEM((d, tn), BF16), pltpu.VMEM((d, tn), BF16)],
        compiler_params=_cparams(("parallel", "arbitrary")),
        name="ffn_up",
    )(u2, wg, wu, cw, cb)


def _ffn_down_kernel(g_ref, wd_ref, h1_ref, nw_ref, o_ref):
    h2 = h1_ref[...] + jnp.dot(g_ref[...], wd_ref[...], preferred_element_type=F32)
    o_ref[...] = _rms_rows(h2, nw_ref[...])


def _ffn_down(gact, wd, h1, nw, bsz, seq, tp, tm):
    m, kdim = gact.shape
    d = wd.shape[1]
    per_seq = seq // tm
    off = SEQ_OFF // tm

    def row_in(i):
        return ((i // per_seq) * (tp // tm) + off + i % per_seq, 0)

    return pl.pallas_call(
        _ffn_down_kernel,
        grid=(bsz * per_seq,),
        in_specs=[pl.BlockSpec((tm, kdim), row_in),
                  pl.BlockSpec((kdim, d), lambda i: (0, 0), pipeline_mode=pl.Buffered(1)),
                  pl.BlockSpec((tm, d), row_in),
                  pl.BlockSpec((1, d), lambda i: (0, 0))],
        out_specs=pl.BlockSpec((tm, d), lambda i: (i, 0)),
        out_shape=jax.ShapeDtypeStruct((bsz * seq, d), F32),
        compiler_params=_cparams(("parallel",)),
        name="ffn_down",
    )(gact, wd, h1, nw)


def kernel(x, meta_tokens, attn_norm_w, w_in, hgrn_lb_logits, hgrn_norm_w, idx_k_norm_w, w_branch_hgrn,
           w_branch_dsa, w_out, ffn_norm_w, w_ffn_gate, w_ffn_up, ffn_conv_w, ffn_conv_b, w_ffn_down,
           final_norm_w):
    bsz, seq, d = x.shape
    depth = w_in.shape[0]
    topk = min(IDX_TOPK_MAX, seq // 4)
    tp = SEQ_OFF + seq
    m = bsz * tp
    hg_w = HG_HEADS * HG_DIM
    at_w = AT_HEADS * AT_HDIM
    n_main_lo = 4 * hg_w + 3 * at_w + IDX_HEADS * IDX_HDIM
    n_small = IDX_HDIM + IDX_HEADS

    prefix = jnp.concatenate([jnp.zeros((META0, d), x.dtype), meta_tokens.astype(x.dtype)], axis=0)
    x2 = x.reshape(bsz * seq, d)
    lbs = jnp.cumsum(jax.nn.softmax(hgrn_lb_logits.astype(F32), axis=0), axis=0)

    assert depth == 1, "multi-layer stacks are not supported"
    out = None
    for l in range(depth):
        wt = jnp.swapaxes(w_in[l], 0, 1)
        w_small = jnp.pad(wt[n_main_lo:n_main_lo + n_small], ((0, LANES - n_small), (0, 0))).T.astype(BF16)
        knw = jnp.pad(idx_k_norm_w[l], (0, LANES - IDX_HDIM)).reshape(1, LANES)

        u, zs, kt, kb = _embed(x2, prefix, attn_norm_w[l].reshape(1, d), w_small, knw, bsz, tp)
        z = _in_proj(u, wt, n_main_lo, n_main_lo + n_small, 2 * d, tm=MM_ROWS, tn=MM_COLS)
        z3 = z.reshape(bsz, tp, z.shape[1])

        y_h = _hgrn(z3, lbs[l].reshape(1, hg_w), hgrn_norm_w[l].reshape(1, HG_DIM))

        wt = zs.reshape(bsz, tp, LANES)[:, :, IDX_HDIM:IDX_HDIM + IDX_HEADS]
        wt = jnp.swapaxes(wt * (IDX_HEADS ** -0.5 * IDX_HDIM ** -0.5), 1, 2)
        bias = _dsa_index(z3, wt, _real_keys(kt.reshape(bsz, tp, LANES)), _real_keys(kb.reshape(bsz, tp, LANES)),
                          topk)
        kk = _real_keys(z3[:, :, 4 * hg_w + at_w:4 * hg_w + 2 * at_w])
        av = _real_keys(z3[:, :, 4 * hg_w + 2 * at_w:4 * hg_w + 3 * at_w])
        vt = jnp.swapaxes(av.reshape(bsz, tp // DSA_KC, DSA_KC, at_w), 2, 3)
        y_a = _dsa_attn(z3, kk, vt, bias)

        mm = _merge(y_h.reshape(m, hg_w), y_a.reshape(m, at_w), z,
                    w_branch_hgrn[l].astype(BF16), w_branch_dsa[l].astype(BF16), tm=MM_ROWS, tn=MM_COLS)
        h1, u2 = _out_proj(x2, prefix, mm, w_out[l].astype(BF16), ffn_norm_w[l].reshape(1, d), bsz, tp)

        cw = jnp.pad(ffn_conv_w[l], ((0, 8 - CONV_W), (0, 0)))
        gact = _ffn_up(u2, w_ffn_gate[l], w_ffn_up[l], cw,
                       ffn_conv_b[l].reshape(1, -1), tm=MM_ROWS, tn=FFN_UP_COLS)
        out = _ffn_down(gact, w_ffn_down[l].astype(BF16), h1, final_norm_w.reshape(1, d),
                        bsz, seq, tp, tm=FFN_DOWN_ROWS)
    return out.reshape(bsz, seq, d)
```

```python
import functools
import math

import jax
import jax.numpy as jnp
import numpy as np
from jax import lax
from jax.experimental import pallas as pl
from jax.experimental.pallas import tpu as pltpu

N_META = 16
EPS = 1e-6
HG_HEADS = 8
HG_DIM = 128
AT_HEADS = 8
AT_HDIM = 128
IDX_HEADS = 16
IDX_HDIM = 64
IDX_TOPK_MAX = 256
CONV_W = 3

LANES = 128
SUBLANES = 8
SEQ_OFF = 512
META0 = SEQ_OFF - N_META
NEG_BIG = -1e30
INT_MIN = -(2 ** 31)

HG_CHUNK = 128
HG_DIAG = 4
SCAN_STEPS = (1, 2, 4)
HG_STEP = 512
HG_HPS = 4
DSA_TQ = 256
DSA_KC = 512
COUNT_ROWS = 64
HALF_OFF = 2 ** 15
ATT_GROUPS = 1
REDUCE_ROWS = 64
ACC_ROWS = AT_HDIM + 16
MM_ROWS = 1024
MM_COLS = 1024
FFN_UP_COLS = 512
FFN_DOWN_ROWS = 256
VMEM_LIMIT = 56 * 1024 * 1024

F32 = jnp.float32
BF16 = jnp.bfloat16
NT_DIMS = (((1,), (1,)), ((), ()))
TN_DIMS = (((0,), (0,)), ((), ()))


def _cparams(sem):
    return pltpu.CompilerParams(dimension_semantics=sem, vmem_limit_bytes=VMEM_LIMIT)


def _rms_rows(x, w):
    ms = jnp.mean(x * x, axis=-1, keepdims=True)
    return x * lax.rsqrt(ms + EPS) * w


def _rows_reduce(x, reduce):
    rows, cols = x.shape
    if rows > REDUCE_ROWS:
        x = reduce(x.reshape(rows // REDUCE_ROWS, REDUCE_ROWS, cols), axis=0)
    return reduce(x, axis=0, keepdims=True)


def _x_block(r, bps):
    return (r // bps) * (bps - 1) + jnp.maximum(r % bps - 1, 0)


def _embed_kernel(x_ref, pre_ref, nw_ref, w_ref, knw_ref, u_ref, zs_ref, kt_ref, kb_ref, *, bps):
    is_prefix = pl.program_id(0) % bps == 0
    h = jnp.where(is_prefix, pre_ref[...], x_ref[...])
    u = _rms_rows(h, nw_ref[...]).astype(BF16)
    u_ref[...] = u
    zs = jnp.dot(u, w_ref[...], preferred_element_type=F32)
    lane = lax.broadcasted_iota(jnp.int32, zs.shape, 1)
    ik = jnp.where(lane < IDX_HDIM, zs, 0.0)
    ms = jnp.sum(ik * ik, axis=-1, keepdims=True) / IDX_HDIM
    kn = ik * lax.rsqrt(ms + EPS) * knw_ref[...]
    zs_ref[...] = zs
    kt_ref[...] = kn.astype(BF16)
    kb_ref[...] = pltpu.roll(kn, IDX_HDIM, axis=1).astype(BF16)


def _embed(x2, prefix, nw, w, knw, bsz, tp):
    d = x2.shape[1]
    tm = SEQ_OFF
    bps = tp // tm
    m = bsz * tp
    row = lambda r: (r, 0)
    fixed = lambda r: (0, 0)
    return pl.pallas_call(
        functools.partial(_embed_kernel, bps=bps),
        grid=(m // tm,),
        in_specs=[pl.BlockSpec((tm, d), lambda r: (_x_block(r, bps), 0)),
                  pl.BlockSpec((tm, d), fixed), pl.BlockSpec((1, d), fixed),
                  pl.BlockSpec((d, LANES), fixed), pl.BlockSpec((1, LANES), fixed)],
        out_specs=[pl.BlockSpec((tm, d), row)] + [pl.BlockSpec((tm, LANES), row)] * 3,
        out_shape=[jax.ShapeDtypeStruct((m, d), BF16),
                   jax.ShapeDtypeStruct((m, LANES), F32),
                   jax.ShapeDtypeStruct((m, LANES), BF16),
                   jax.ShapeDtypeStruct((m, LANES), BF16)],
        compiler_params=_cparams(("parallel",)),
        name="embed_norm",
    )(x2, prefix, nw, w, knw)


def _in_proj_kernel(u_ref, wt_ref, o_ref, w_scr):
    @pl.when(pl.program_id(1) == 0)
    def _():
        w_scr[...] = wt_ref[...].astype(BF16)

    o_ref[...] = lax.dot_general(u_ref[...], w_scr[...], NT_DIMS,
                                 preferred_element_type=F32).astype(o_ref.dtype)


def _in_proj(u, wt, n_lo_cols, hi_start, n_hi_cols, tm, tn):
    m, d = u.shape
    n_lo = n_lo_cols // tn
    n_blocks = n_lo + n_hi_cols // tn

    assert hi_start % SUBLANES == 0

    def w_rows(j, i):
        return (pl.multiple_of(jnp.where(j < n_lo, j * tn, hi_start + (j - n_lo) * tn), SUBLANES), 0)

    return pl.pallas_call(
        _in_proj_kernel,
        grid=(n_blocks, m // tm),
        in_specs=[pl.BlockSpec((tm, d), lambda j, i: (i, 0)),
                  pl.BlockSpec((pl.Element(tn), pl.Element(d)), w_rows)],
        out_specs=pl.BlockSpec((tm, tn), lambda j, i: (i, j)),
        out_shape=jax.ShapeDtypeStruct((m, n_blocks * tn), BF16),
        scratch_shapes=[pltpu.VMEM((tn, d), BF16)],
        compiler_params=_cparams(("parallel", "arbitrary")),
        name="in_proj",
    )(u, wt)


def _hgrn_masks():
    c = HG_CHUNK
    t = np.arange(c)[:, None]
    s = np.arange(c)[None, :]
    scan = [np.broadcast_to((t % 8) >= k, (c, c)) for k in SCAN_STEPS]
    levels = [(t // (2 * m) == s // (2 * m)) & ((t // m) % 2 == 1) & ((s // m) % 2 == 0) for m in _hgrn_levels()]
    diags = [(s == t - d) & ((t % HG_DIAG) >= d) for d in range(HG_DIAG)]
    return np.stack(scan + levels + diags).astype(np.float32)


def _hgrn_levels():
    m, out = HG_CHUNK // 2, []
    while m >= HG_DIAG:
        out.append(m)
        m //= 2
    return out


def _sigmoid(x):
    return 1.0 / (1.0 + jnp.exp(-x))


def _hgrn_kernel(q_ref, f_ref, i_ref, g_ref, lb_ref, nw_ref, mk_ref, o_ref, st_ref):
    c = HG_CHUNK
    levels = _hgrn_levels()
    n_scan = len(SCAN_STEPS)

    @pl.when(pl.program_id(2) == 0)
    def _():
        st_ref[...] = jnp.zeros_like(st_ref)

    def chunk(ci, hh, st):
        rows = pl.ds(ci * c, c)
        cols = slice(hh * HG_DIM, (hh + 1) * HG_DIM)
        lb = lb_ref[:, cols]
        oml = 1.0 - lb
        qz = q_ref[rows, cols].astype(F32)
        fz = f_ref[rows, cols].astype(F32)
        v = i_ref[rows, cols]
        gz = g_ref[rows, cols].astype(F32)

        q = qz * _sigmoid(qz)
        sf = _sigmoid(fz)
        f = lb + oml * sf
        kc = oml * (1.0 - sf)

        bw = jnp.log2(f)
        for k, step in enumerate(SCAN_STEPS):
            bw = bw + pltpu.roll(bw, step, axis=0) * mk_ref[k]
        run = jnp.zeros((1, HG_DIM), F32)
        tiles = []
        for i in range(c // 8):
            tiles.append(bw[8 * i:8 * i + 8, :] + run)
            run = run + bw[8 * i + 7:8 * i + 8, :]
        b = jnp.concatenate(tiles, axis=0)

        a = jnp.zeros((c, c), F32)
        for k, m in enumerate(levels):
            piv = jnp.concatenate(
                [jnp.broadcast_to(b[blk * 2 * m + m - 1: blk * 2 * m + m, :], (2 * m, HG_DIM))
                 for blk in range(c // (2 * m))], axis=0)
            x = jnp.exp2(-jnp.abs(b - piv))
            am = lax.dot_general((q * x).astype(BF16), (kc * x).astype(BF16), NT_DIMS,
                                 preferred_element_type=F32)
            a = a + am * mk_ref[n_scan + k]

        gd = None
        for d in range(HG_DIAG):
            if d == 0:
                xd = q * kc
            else:
                fd = f if d == 1 else pltpu.roll(f, d - 1, axis=0)
                gd = fd if gd is None else gd * fd
                xd = q * pltpu.roll(kc, d, axis=0) * gd
            a = a + jnp.sum(xd, axis=-1, keepdims=True) * mk_ref[n_scan + len(levels) + d]

        qd = (q * jnp.exp2(b)).astype(BF16)
        o = lax.dot_general(qd, st.astype(BF16), NT_DIMS, preferred_element_type=F32)
        o = o + jnp.dot(a.astype(BF16), v, preferred_element_type=F32)

        b_last = b[c - 1:c, :]
        kd = (kc * jnp.exp2(b_last - b)).astype(BF16)
        st = st * jnp.exp2(b_last) + lax.dot_general(v, kd, TN_DIMS, preferred_element_type=F32)

        y = _rms_rows(o, nw_ref[...]) * (gz * _sigmoid(gz))
        o_ref[rows, cols] = y.astype(o_ref.dtype)
        return st

    sts = [st_ref[hh] for hh in range(HG_HPS)]
    for ci in range(q_ref.shape[0] // c):
        for hh in range(HG_HPS):
            sts[hh] = chunk(ci, hh, sts[hh])
    for hh in range(HG_HPS):
        st_ref[hh] = sts[hh]


def _hgrn(z3, lb, nw):
    bsz, tp, _ = z3.shape
    masks = jnp.asarray(_hgrn_masks())

    hw = HG_HPS * HG_DIM
    hgroups = HG_HEADS // HG_HPS

    def zspec(group):
        return pl.BlockSpec((None, HG_STEP, hw), lambda b, h, s: (b, s, group * hgroups + h))

    return pl.pallas_call(
        _hgrn_kernel,
        grid=(bsz, hgroups, tp // HG_STEP),
        in_specs=[zspec(0), zspec(1), zspec(2), zspec(3),
                  pl.BlockSpec((1, hw), lambda b, h, s: (0, h)),
                  pl.BlockSpec((1, HG_DIM), lambda b, h, s: (0, 0)),
                  pl.BlockSpec(masks.shape, lambda b, h, s: (0, 0, 0))],
        out_specs=pl.BlockSpec((None, HG_STEP, hw), lambda b, h, s: (b, s, h)),
        out_shape=jax.ShapeDtypeStruct((bsz, tp, HG_HEADS * HG_DIM), BF16),
        scratch_shapes=[pltpu.VMEM((HG_HPS, HG_DIM, HG_DIM), F32)],
        compiler_params=_cparams(("parallel", "parallel", "arbitrary")),
        name="hgrn2",
    )(z3, z3, z3, z3, lb, nw, masks)


def _causal_chunks(q0, tq):
    return jnp.maximum(q0 + tq - 1 - META0, 0) // DSA_KC + 1


def _chunk_row(ci, tp):
    return pl.multiple_of(jnp.minimum(META0 + ci * DSA_KC, tp - DSA_KC), 16)


def _chunked_keys(a):
    tp = a.shape[1]
    n_full = (tp - META0) // DSA_KC
    return jnp.concatenate([a[:, META0:META0 + n_full * DSA_KC], a[:, tp - DSA_KC:]], axis=1)


def _dsa_index_kernel(iq_ref, wt_ref, kt_ref, kb_ref, bias_ref, keys_ref, hi_ref, lo_ref, acc_ref, *, topk):
    tq = iq_ref.shape[0]
    kc = DSA_KC
    tp = kt_ref.shape[0]
    q0 = pl.program_id(1) * tq
    nchunks = _causal_chunks(q0, tq)
    qpos = q0 - META0 + lax.broadcasted_iota(jnp.int32, (1, tq), 1)
    int_min = jnp.int32(INT_MIN)
    half = jnp.int32(HALF_OFF)

    def score_chunk(ci, carry):
        k0 = pl.multiple_of(ci * kc, kc)
        r0 = _chunk_row(ci, tp)
        kt = kt_ref[pl.ds(r0, kc), :]
        kb = kb_ref[pl.ds(r0, kc), :]
        for p in range(IDX_HEADS // 2):
            qp = iq_ref[:, p * LANES:(p + 1) * LANES]
            st = lax.dot_general(kt, qp, NT_DIMS, preferred_element_type=F32)
            sb = lax.dot_general(kb, qp, NT_DIMS, preferred_element_type=F32)
            part = (wt_ref[2 * p:2 * p + 1, :] * jnp.maximum(st, 0.0)
                    + wt_ref[2 * p + 1:2 * p + 2, :] * jnp.maximum(sb, 0.0))
            if p == 0:
                acc_ref[...] = part
            else:
                acc_ref[...] += part
        score = acc_ref[...] + 0.0
        bits = lax.bitcast_convert_type(score, jnp.int32)
        key = bits ^ ((bits >> 31) & jnp.int32(0x7FFFFFFF))
        kpos = r0 - META0 + lax.broadcasted_iota(jnp.int32, (kc, 1), 0)
        valid = (kpos >= k0) & (kpos <= qpos)
        key = jnp.where(valid, key, int_min)
        keys_ref[pl.ds(k0, kc), :] = key
        hi_ref[pl.ds(k0, kc), :] = (key >> 16).astype(jnp.int16)
        lo_ref[pl.ds(k0, kc), :] = (((key << 16) ^ int_min) >> 16).astype(jnp.int16)
        return carry

    lax.fori_loop(0, nchunks, score_chunk, 0)

    def count_ge(src_ref, cand_u):
        cand = (cand_u - half).astype(jnp.int16)

        def count_chunk(ci, cnt):
            k0 = pl.multiple_of(ci * kc, kc)
            for r in range(kc // COUNT_ROWS):
                ks = src_ref[pl.ds(k0 + r * COUNT_ROWS, COUNT_ROWS), :]
                cnt = cnt + jnp.where(ks >= cand, jnp.int16(1), jnp.int16(0))
            return cnt

        cnt = lax.fori_loop(0, nchunks, count_chunk, jnp.zeros((COUNT_ROWS, tq), jnp.int16))
        return jnp.sum(cnt.astype(F32), axis=0, keepdims=True)

    def bisect16(src_ref, need, count_at_zero):
        def bit_pass(i, carry):
            u, at_u, above_u = carry
            cand_u = u | lax.shift_left(jnp.int32(1), 15 - i)
            c = count_ge(src_ref, cand_u)
            ok = c >= need
            return jnp.where(ok, cand_u, u), jnp.where(ok, c, at_u), jnp.where(ok, above_u, c)
        init = (jnp.zeros((1, tq), jnp.int32), count_at_zero, jnp.zeros((1, tq), F32))
        return lax.fori_loop(0, 16, bit_pass, init)

    kf = jnp.full((1, tq), float(topk), F32)
    u_hi, n_hi, above = bisect16(hi_ref, kf, kf)
    hi_thr = (u_hi - half).astype(jnp.int16)

    def narrow_chunk(ci, carry):
        rows = pl.ds(pl.multiple_of(ci * kc, kc), kc)
        lo_ref[rows, :] = jnp.where(hi_ref[rows, :] == hi_thr, lo_ref[rows, :], jnp.int16(-HALF_OFF))
        return carry

    lax.fori_loop(0, nchunks, narrow_chunk, 0)
    u_lo, n_lo, _ = bisect16(lo_ref, topk - above, n_hi - above)
    few = u_hi == 0
    thr = jnp.where(few, int_min + 1, ((u_hi - half) << 16) | u_lo)
    tied = jnp.where(few, 0.0, above + n_lo - topk)
    any_tied = jnp.max(tied) > 0.0

    @pl.when(jnp.logical_not(any_tied))
    def _():
        def write_chunk(ci, carry):
            k0 = pl.multiple_of(ci * kc, kc)
            ks = keys_ref[pl.ds(k0, kc), :]
            bias_ref[pl.ds(k0, kc), :] = jnp.where(ks >= thr, 0.0, NEG_BIG).astype(bias_ref.dtype)
            return carry

        lax.fori_loop(0, nchunks, write_chunk, 0)

    @pl.when(any_tied)
    def _():
        def count_gt(ci, cnt):
            k0 = pl.multiple_of(ci * kc, kc)
            return cnt + jnp.sum(jnp.where(keys_ref[pl.ds(k0, kc), :] > thr, 1.0, 0.0), axis=0, keepdims=True)

        room = topk - lax.fori_loop(0, nchunks, count_gt, jnp.zeros((1, tq), F32))
        lower_tri = (lax.broadcasted_iota(jnp.int32, (kc, kc), 0)
                     >= lax.broadcasted_iota(jnp.int32, (kc, kc), 1)).astype(BF16)

        def write_chunk(ci, seen):
            k0 = pl.multiple_of(ci * kc, kc)
            ks = keys_ref[pl.ds(k0, kc), :]
            eq = ks == thr
            rank = seen + jnp.dot(lower_tri, jnp.where(eq, 1.0, 0.0).astype(BF16), preferred_element_type=F32)
            sel = (ks > thr) | (eq & (rank <= room))
            bias_ref[pl.ds(k0, kc), :] = jnp.where(sel, 0.0, NEG_BIG).astype(bias_ref.dtype)
            return rank[kc - 1:kc, :]

        lax.fori_loop(0, nchunks, write_chunk, jnp.zeros((1, tq), F32))

    def fill_chunk(ci, carry):
        k0 = pl.multiple_of(ci * kc, kc)
        bias_ref[pl.ds(k0, kc), :] = jnp.full((kc, tq), NEG_BIG, bias_ref.dtype)
        return carry

    lax.fori_loop(nchunks, tp // kc, fill_chunk, 0)


def _dsa_index(z3, wt, kt, kb, topk):
    bsz, tp, _ = z3.shape
    iq_col = (HG_HEADS * HG_DIM * 4 + AT_HEADS * AT_HDIM * 3) // (IDX_HEADS * IDX_HDIM)
    return pl.pallas_call(
        functools.partial(_dsa_index_kernel, topk=topk),
        grid=(bsz, tp // DSA_TQ),
        in_specs=[pl.BlockSpec((None, DSA_TQ, IDX_HEADS * IDX_HDIM), lambda b, i: (b, i, iq_col)),
                  pl.BlockSpec((None, IDX_HEADS, DSA_TQ), lambda b, i: (b, 0, i)),
                  pl.BlockSpec((None, tp, LANES), lambda b, i: (b, 0, 0)),
                  pl.BlockSpec((None, tp, LANES), lambda b, i: (b, 0, 0))],
        out_specs=pl.BlockSpec((None, None, tp, DSA_TQ), lambda b, i: (b, i, 0, 0)),
        out_shape=jax.ShapeDtypeStruct((bsz, tp // DSA_TQ, tp, DSA_TQ), BF16),
        scratch_shapes=[pltpu.VMEM((tp, DSA_TQ), jnp.int32), pltpu.VMEM((tp, DSA_TQ), jnp.int16),
                        pltpu.VMEM((tp, DSA_TQ), jnp.int16), pltpu.VMEM((DSA_KC, DSA_TQ), F32)],
        compiler_params=_cparams(("parallel", "arbitrary")),
        name="dsa_index",
    )(z3, wt, kt, kb)


def _dsa_attn_kernel(aq_ref, k_ref, vt_ref, bias_ref, o_ref, qs_ref, acc_ref, lg_ref, p_ref):
    tq = aq_ref.shape[0]
    kc = DSA_KC
    heads = aq_ref.shape[1] // AT_HDIM
    q0 = pl.program_id(2) * tq
    nchunks = _causal_chunks(q0, tq)

    qs_ref[...] = (aq_ref[...].astype(F32) * (AT_HDIM ** -0.5 * math.log2(math.e))).astype(qs_ref.dtype)
    acc_ref[...] = jnp.zeros(acc_ref.shape, F32)
    ones_rows = (lax.broadcasted_iota(jnp.int32, (ACC_ROWS - AT_HDIM, kc), 0) == 0).astype(BF16)

    def chunk(ci, ms):
        k0 = pl.multiple_of(ci * kc, kc)
        r0 = _chunk_row(ci, k_ref.shape[0])
        bias = bias_ref[pl.ds(k0, kc), :]
        new_ms = []
        for h in range(heads):
            hs = slice(h * AT_HDIM, (h + 1) * AT_HDIM)
            lg = lax.dot_general(k_ref[pl.ds(r0, kc), hs], qs_ref[:, hs], NT_DIMS,
                                 preferred_element_type=F32).astype(BF16) + bias
            lg_ref[h] = lg
            new_ms.append(jnp.maximum(ms[h], _rows_reduce(lg, jnp.max)))
        for h in range(heads):
            hs = slice(h * AT_HDIM, (h + 1) * AT_HDIM)
            alpha = jnp.exp2(ms[h].astype(F32) - new_ms[h].astype(F32))
            p_ref[h] = jnp.exp2(lg_ref[h] - new_ms[h])
            v_ext = jnp.concatenate([vt_ref[ci, hs, :], ones_rows], axis=0)
            acc_ref[h] = alpha * acc_ref[h] + jnp.dot(v_ext, p_ref[h], preferred_element_type=F32)
        return tuple(new_ms)

    init = tuple(jnp.full((1, tq), NEG_BIG, BF16) for _ in range(heads))
    lax.fori_loop(0, nchunks, chunk, init)

    live = (q0 + lax.broadcasted_iota(jnp.int32, (1, tq), 1)) >= META0
    for h in range(heads):
        num = acc_ref[h, 0:AT_HDIM, :]
        den = acc_ref[h, AT_HDIM:AT_HDIM + 1, :]
        out_t = jnp.where(live, num / den, 0.0)
        o_ref[:, h * AT_HDIM:(h + 1) * AT_HDIM] = out_t.T.astype(o_ref.dtype)


def _dsa_attn(z3, vt, bias):
    bsz, tp, _ = z3.shape
    gw = AT_HEADS * AT_HDIM // ATT_GROUPS
    aq_col = HG_HEADS * HG_DIM * 4 // gw
    ak_col = aq_col + ATT_GROUPS
    hpg = AT_HEADS // ATT_GROUPS
    return pl.pallas_call(
        _dsa_attn_kernel,
        grid=(bsz, ATT_GROUPS, tp // DSA_TQ),
        in_specs=[pl.BlockSpec((None, DSA_TQ, gw), lambda b, g, i: (b, i, aq_col + g)),
                  pl.BlockSpec((None, tp, gw), lambda b, g, i: (b, 0, ak_col + g), pipeline_mode=pl.Buffered(1)),
                  pl.BlockSpec((None, tp // DSA_KC, gw, DSA_KC), lambda b, g, i: (b, 0, g, 0),
                               pipeline_mode=pl.Buffered(1)),
                  pl.BlockSpec((None, None, tp, DSA_TQ), lambda b, g, i: (b, i, 0, 0))],
        out_specs=pl.BlockSpec((None, DSA_TQ, gw), lambda b, g, i: (b, i, g)),
        out_shape=jax.ShapeDtypeStruct((bsz, tp, AT_HEADS * AT_HDIM), BF16),
        scratch_shapes=[pltpu.VMEM((DSA_TQ, gw), BF16),
                        pltpu.VMEM((hpg, ACC_ROWS, DSA_TQ), F32),
                        pltpu.VMEM((hpg, DSA_KC, DSA_TQ), BF16),
                        pltpu.VMEM((hpg, DSA_KC, DSA_TQ), BF16)],
        compiler_params=_cparams(("parallel", "parallel", "arbitrary")),
        name="dsa_attn",
    )(z3, z3, vt, bias)


def _merge_kernel(yh_ref, ya_ref, ga_ref, gb_ref, wh_ref, wa_ref, o_ref):
    mh = jnp.dot(yh_ref[...], wh_ref[...], preferred_element_type=F32)
    ma = jnp.dot(ya_ref[...], wa_ref[...], preferred_element_type=F32)
    o = jax.nn.sigmoid(ga_ref[...].astype(F32)) * mh + jax.nn.sigmoid(gb_ref[...].astype(F32)) * ma
    o_ref[...] = o.astype(o_ref.dtype)


def _merge(yh, ya, z, wh, wa, tm, tn):
    m, kdim = yh.shape
    n = wh.shape[1]
    ga_col = (z.shape[1] - 2 * n) // tn
    gb_col = (z.shape[1] - n) // tn
    return pl.pallas_call(
        _merge_kernel,
        grid=(m // tm, n // tn),
        in_specs=[pl.BlockSpec((tm, kdim), lambda i, j: (i, 0)),
                  pl.BlockSpec((tm, kdim), lambda i, j: (i, 0)),
                  pl.BlockSpec((tm, tn), lambda i, j: (i, ga_col + j)),
                  pl.BlockSpec((tm, tn), lambda i, j: (i, gb_col + j)),
                  pl.BlockSpec((kdim, tn), lambda i, j: (0, j)),
                  pl.BlockSpec((kdim, tn), lambda i, j: (0, j))],
        out_specs=pl.BlockSpec((tm, tn), lambda i, j: (i, j)),
        out_shape=jax.ShapeDtypeStruct((m, n), BF16),
        compiler_params=_cparams(("parallel", "arbitrary")),
        name="branch_merge",
    )(yh, ya, z, z, wh, wa)


def _out_proj_kernel(x_ref, pre_ref, m_ref, w_ref, nw_ref, h1_ref, u2_ref, *, bps):
    is_prefix = pl.program_id(0) % bps == 0
    h = jnp.where(is_prefix, pre_ref[...], x_ref[...])
    h1 = h + jnp.dot(m_ref[...], w_ref[...], preferred_element_type=F32)
    h1_ref[...] = h1
    u2_ref[...] = _rms_rows(h1, nw_ref[...]).astype(u2_ref.dtype)


def _out_proj(x2, prefix, mm, w, nw, bsz, tp):
    m, d = mm.shape
    tm = SEQ_OFF
    bps = tp // tm
    row = lambda r: (r, 0)
    fixed = lambda r: (0, 0)
    return pl.pallas_call(
        functools.partial(_out_proj_kernel, bps=bps),
        grid=(m // tm,),
        in_specs=[pl.BlockSpec((tm, d), lambda r: (_x_block(r, bps), 0)), pl.BlockSpec((tm, d), fixed),
                  pl.BlockSpec((tm, d), row), pl.BlockSpec((d, d), fixed), pl.BlockSpec((1, d), fixed)],
        out_specs=[pl.BlockSpec((tm, d), row), pl.BlockSpec((tm, d), row)],
        out_shape=[jax.ShapeDtypeStruct((m, d), F32), jax.ShapeDtypeStruct((m, d), BF16)],
        compiler_params=_cparams(("parallel",)),
        name="out_proj",
    )(x2, prefix, mm, w, nw)


def _ffn_up_kernel(u_ref, wg_ref, wu_ref, cw_ref, cb_ref, o_ref, tail_ref, wg_scr, wu_scr):
    tm = u_ref.shape[0]

    @pl.when(pl.program_id(1) == 0)
    def _():
        tail_ref[...] = jnp.zeros_like(tail_ref)
        wg_scr[...] = wg_ref[...].astype(BF16)
        wu_scr[...] = wu_ref[...].astype(BF16)

    u = u_ref[...]
    g0 = jnp.dot(u, wg_scr[...], preferred_element_type=F32)
    up = jnp.dot(u, wu_scr[...], preferred_element_type=F32)
    row = lax.broadcasted_iota(jnp.int32, (tm, 1), 0)
    prev1 = tail_ref[7:8, :]
    prev2 = tail_ref[6:7, :]
    g1 = jnp.where(row == 0, prev1, pltpu.roll(g0, 1, axis=0))
    g2 = jnp.where(row == 0, prev2, jnp.where(row == 1, prev1, pltpu.roll(g0, 2, axis=0)))
    tail_ref[...] = g0[tm - 8:tm, :]
    a = cw_ref[0:1, :] * g2 + cw_ref[1:2, :] * g1 + cw_ref[2:3, :] * g0 + cb_ref[...]
    o_ref[...] = (a * jax.nn.sigmoid(a) * up).astype(o_ref.dtype)


def _ffn_up(u2, wg, wu, cw, cb, tm, tn):
    m, d = u2.shape
    n = wg.shape[1]
    return pl.pallas_call(
        _ffn_up_kernel,
        grid=(n // tn, m // tm),
        in_specs=[pl.BlockSpec((tm, d), lambda j, i: (i, 0)),
                  pl.BlockSpec((d, tn), lambda j, i: (0, j)),
                  pl.BlockSpec((d, tn), lambda j, i: (0, j)),
                  pl.BlockSpec((8, tn), lambda j, i: (0, j)),
                  pl.BlockSpec((1, tn), lambda j, i: (0, j))],
        out_specs=pl.BlockSpec((tm, tn), lambda j, i: (i, j)),
        out_shape=jax.ShapeDtypeStruct((m, n), BF16),
        scratch_shapes=[pltpu.VMEM((8, tn), F32), pltpu.VMEM((d, tn), BF16), pltpu.VMEM((d, tn), BF16)],
        compiler_params=_cparams(("parallel", "arbitrary")),
        name="ffn_up",
    )(u2, wg, wu, cw, cb)


def _ffn_down_kernel(g_ref, wd_ref, h1_ref, nw_ref, o_ref):
    h2 = h1_ref[...] + jnp.dot(g_ref[...], wd_ref[...], preferred_element_type=F32)
    o_ref[...] = _rms_rows(h2, nw_ref[...])


def _ffn_down(gact, wd, h1, nw, bsz, seq, tp, tm):
    m, kdim = gact.shape
    d = wd.shape[1]
    per_seq = seq // tm
    off = SEQ_OFF // tm

    def row_in(i):
        return ((i // per_seq) * (tp // tm) + off + i % per_seq, 0)

    return pl.pallas_call(
        _ffn_down_kernel,
        grid=(bsz * per_seq,),
        in_specs=[pl.BlockSpec((tm, kdim), row_in),
                  pl.BlockSpec((kdim, d), lambda i: (0, 0), pipeline_mode=pl.Buffered(1)),
                  pl.BlockSpec((tm, d), row_in),
                  pl.BlockSpec((1, d), lambda i: (0, 0))],
        out_specs=pl.BlockSpec((tm, d), lambda i: (i, 0)),
        out_shape=jax.ShapeDtypeStruct((bsz * seq, d), F32),
        compiler_params=_cparams(("parallel",)),
        name="ffn_down",
    )(gact, wd, h1, nw)


def kernel(x, meta_tokens, attn_norm_w, w_in, hgrn_lb_logits, hgrn_norm_w, idx_k_norm_w, w_branch_hgrn,
           w_branch_dsa, w_out, ffn_norm_w, w_ffn_gate, w_ffn_up, ffn_conv_w, ffn_conv_b, w_ffn_down,
           final_norm_w):
    bsz, seq, d = x.shape
    depth = w_in.shape[0]
    topk = min(IDX_TOPK_MAX, seq // 4)
    tp = SEQ_OFF + seq
    m = bsz * tp
    hg_w = HG_HEADS * HG_DIM
    at_w = AT_HEADS * AT_HDIM
    n_main_lo = 4 * hg_w + 3 * at_w + IDX_HEADS * IDX_HDIM
    n_small = IDX_HDIM + IDX_HEADS

    prefix = jnp.concatenate([jnp.zeros((META0, d), x.dtype), meta_tokens.astype(x.dtype)], axis=0)
    x2 = x.reshape(bsz * seq, d)
    lbs = jnp.cumsum(jax.nn.softmax(hgrn_lb_logits.astype(F32), axis=0), axis=0)

    assert depth == 1, "multi-layer stacks are not supported"
    out = None
    for l in range(depth):
        wt = jnp.swapaxes(w_in[l], 0, 1)
        w_small = jnp.pad(wt[n_main_lo:n_main_lo + n_small], ((0, LANES - n_small), (0, 0))).T.astype(BF16)
        knw = jnp.pad(idx_k_norm_w[l], (0, LANES - IDX_HDIM)).reshape(1, LANES)

        u, zs, kt, kb = _embed(x2, prefix, attn_norm_w[l].reshape(1, d), w_small, knw, bsz, tp)
        z = _in_proj(u, wt, n_main_lo, n_main_lo + n_small, 2 * d, tm=MM_ROWS, tn=MM_COLS)
        z3 = z.reshape(bsz, tp, z.shape[1])

        y_h = _hgrn(z3, lbs[l].reshape(1, hg_w), hgrn_norm_w[l].reshape(1, HG_DIM))

        wt = zs.reshape(bsz, tp, LANES)[:, :, IDX_HDIM:IDX_HDIM + IDX_HEADS]
        wt = jnp.swapaxes(wt * (IDX_HEADS ** -0.5 * IDX_HDIM ** -0.5), 1, 2)
        bias = _dsa_index(z3, wt, kt.reshape(bsz, tp, LANES), kb.reshape(bsz, tp, LANES), topk)
        av = _chunked_keys(z3[:, :, 4 * hg_w + 2 * at_w:4 * hg_w + 3 * at_w])
        vt = jnp.swapaxes(av.reshape(bsz, tp // DSA_KC, DSA_KC, at_w), 2, 3)
        y_a = _dsa_attn(z3, vt, bias)

        mm = _merge(y_h.reshape(m, hg_w), y_a.reshape(m, at_w), z,
                    w_branch_hgrn[l].astype(BF16), w_branch_dsa[l].astype(BF16), tm=MM_ROWS, tn=MM_COLS)
        h1, u2 = _out_proj(x2, prefix, mm, w_out[l].astype(BF16), ffn_norm_w[l].reshape(1, d), bsz, tp)

        cw = jnp.pad(ffn_conv_w[l], ((0, 8 - CONV_W), (0, 0)))
        gact = _ffn_up(u2, w_ffn_gate[l], w_ffn_up[l], cw,
                       ffn_conv_b[l].reshape(1, -1), tm=MM_ROWS, tn=FFN_UP_COLS)
        out = _ffn_down(gact, w_ffn_down[l].astype(BF16), h1, final_norm_w.reshape(1, d),
                        bsz, seq, tp, tm=FFN_DOWN_ROWS)
    return out.reshape(bsz, seq, d)
```

```python
import functools
import math

import jax
import jax.numpy as jnp
import numpy as np
from jax import lax
from jax.experimental import pallas as pl
from jax.experimental.pallas import tpu as pltpu

N_META = 16
EPS = 1e-6
HG_HEADS = 8
HG_DIM = 128
AT_HEADS = 8
AT_HDIM = 128
IDX_HEADS = 16
IDX_HDIM = 64
IDX_TOPK_MAX = 256
CONV_W = 3

LANES = 128
SUBLANES = 8
SEQ_OFF = 512
META0 = SEQ_OFF - N_META
NEG_BIG = -1e30
INT_MIN = -(2 ** 31)

HG_CHUNK = 128
HG_DIAG = 4
SCAN_STEPS = (1, 2, 4)
HG_STEP = 512
HG_HPS = 8
DSA_TQ = 256
DSA_KC = 512
COUNT_ROWS = 64
HALF_OFF = 2 ** 15
ATT_GROUPS = 1
REDUCE_ROWS = 64
ACC_ROWS = AT_HDIM + 16
MM_ROWS = 1024
MM_COLS = 1024
FFN_UP_COLS = 512
FFN_DOWN_ROWS = 256
VMEM_LIMIT = 56 * 1024 * 1024

F32 = jnp.float32
BF16 = jnp.bfloat16
NT_DIMS = (((1,), (1,)), ((), ()))
TN_DIMS = (((0,), (0,)), ((), ()))


def _cparams(sem):
    return pltpu.CompilerParams(dimension_semantics=sem, vmem_limit_bytes=VMEM_LIMIT)


def _rms_rows(x, w):
    ms = jnp.mean(x * x, axis=-1, keepdims=True)
    return x * lax.rsqrt(ms + EPS) * w


def _rows_reduce(x, reduce):
    rows, cols = x.shape
    if rows > REDUCE_ROWS:
        x = reduce(x.reshape(rows // REDUCE_ROWS, REDUCE_ROWS, cols), axis=0)
    return reduce(x, axis=0, keepdims=True)


def _x_block(r, bps):
    return (r // bps) * (bps - 1) + jnp.maximum(r % bps - 1, 0)


def _embed_kernel(x_ref, pre_ref, nw_ref, w_ref, knw_ref, u_ref, zs_ref, kt_ref, kb_ref, *, bps):
    is_prefix = pl.program_id(0) % bps == 0
    h = jnp.where(is_prefix, pre_ref[...], x_ref[...])
    u = _rms_rows(h, nw_ref[...]).astype(BF16)
    u_ref[...] = u
    zs = jnp.dot(u, w_ref[...], preferred_element_type=F32)
    lane = lax.broadcasted_iota(jnp.int32, zs.shape, 1)
    ik = jnp.where(lane < IDX_HDIM, zs, 0.0)
    ms = jnp.sum(ik * ik, axis=-1, keepdims=True) / IDX_HDIM
    kn = ik * lax.rsqrt(ms + EPS) * knw_ref[...]
    zs_ref[...] = zs
    kt_ref[...] = kn.astype(BF16)
    kb_ref[...] = pltpu.roll(kn, IDX_HDIM, axis=1).astype(BF16)


def _embed(x2, prefix, nw, w, knw, bsz, tp):
    d = x2.shape[1]
    tm = SEQ_OFF
    bps = tp // tm
    m = bsz * tp
    row = lambda r: (r, 0)
    fixed = lambda r: (0, 0)
    return pl.pallas_call(
        functools.partial(_embed_kernel, bps=bps),
        grid=(m // tm,),
        in_specs=[pl.BlockSpec((tm, d), lambda r: (_x_block(r, bps), 0)),
                  pl.BlockSpec((tm, d), fixed), pl.BlockSpec((1, d), fixed),
                  pl.BlockSpec((d, LANES), fixed), pl.BlockSpec((1, LANES), fixed)],
        out_specs=[pl.BlockSpec((tm, d), row)] + [pl.BlockSpec((tm, LANES), row)] * 3,
        out_shape=[jax.ShapeDtypeStruct((m, d), BF16),
                   jax.ShapeDtypeStruct((m, LANES), F32),
                   jax.ShapeDtypeStruct((m, LANES), BF16),
                   jax.ShapeDtypeStruct((m, LANES), BF16)],
        compiler_params=_cparams(("parallel",)),
        name="embed_norm",
    )(x2, prefix, nw, w, knw)


def _in_proj_kernel(u_ref, wt_ref, o_ref, w_scr):
    @pl.when(pl.program_id(1) == 0)
    def _():
        w_scr[...] = wt_ref[...].astype(BF16)

    o_ref[...] = lax.dot_general(u_ref[...], w_scr[...], NT_DIMS,
                                 preferred_element_type=F32).astype(o_ref.dtype)


def _in_proj(u, wt, n_lo_cols, hi_start, n_hi_cols, tm, tn):
    m, d = u.shape
    n_lo = n_lo_cols // tn
    n_blocks = n_lo + n_hi_cols // tn

    assert hi_start % SUBLANES == 0

    def w_rows(j, i):
        return (pl.multiple_of(jnp.where(j < n_lo, j * tn, hi_start + (j - n_lo) * tn), SUBLANES), 0)

    return pl.pallas_call(
        _in_proj_kernel,
        grid=(n_blocks, m // tm),
        in_specs=[pl.BlockSpec((tm, d), lambda j, i: (i, 0)),
                  pl.BlockSpec((pl.Element(tn), pl.Element(d)), w_rows)],
        out_specs=pl.BlockSpec((tm, tn), lambda j, i: (i, j)),
        out_shape=jax.ShapeDtypeStruct((m, n_blocks * tn), BF16),
        scratch_shapes=[pltpu.VMEM((tn, d), BF16)],
        compiler_params=_cparams(("parallel", "arbitrary")),
        name="in_proj",
    )(u, wt)


def _hgrn_masks():
    c = HG_CHUNK
    t = np.arange(c)[:, None]
    s = np.arange(c)[None, :]
    scan = [np.broadcast_to((t % 8) >= k, (c, c)) for k in SCAN_STEPS]
    levels = [(t // (2 * m) == s // (2 * m)) & ((t // m) % 2 == 1) & ((s // m) % 2 == 0) for m in _hgrn_levels()]
    diags = [(s == t - d) & ((t % HG_DIAG) >= d) for d in range(HG_DIAG)]
    return np.stack(scan + levels + diags).astype(np.float32)


def _hgrn_levels():
    m, out = HG_CHUNK // 2, []
    while m >= HG_DIAG:
        out.append(m)
        m //= 2
    return out


def _sigmoid(x):
    return 1.0 / (1.0 + jnp.exp(-x))


def _hgrn_kernel(q_ref, f_ref, i_ref, g_ref, lb_ref, nw_ref, mk_ref, o_ref, st_ref):
    c = HG_CHUNK
    levels = _hgrn_levels()
    n_scan = len(SCAN_STEPS)

    @pl.when(pl.program_id(2) == 0)
    def _():
        st_ref[...] = jnp.zeros_like(st_ref)

    def chunk(ci, hh, st):
        rows = pl.ds(ci * c, c)
        cols = slice(hh * HG_DIM, (hh + 1) * HG_DIM)
        lb = lb_ref[:, cols]
        oml = 1.0 - lb
        qz = q_ref[rows, cols].astype(F32)
        fz = f_ref[rows, cols].astype(F32)
        v = i_ref[rows, cols]
        gz = g_ref[rows, cols].astype(F32)

        q = qz * _sigmoid(qz)
        sf = _sigmoid(fz)
        f = lb + oml * sf
        kc = oml * (1.0 - sf)

        bw = jnp.log2(f)
        for k, step in enumerate(SCAN_STEPS):
            bw = bw + pltpu.roll(bw, step, axis=0) * mk_ref[k]
        run = jnp.zeros((1, HG_DIM), F32)
        tiles = []
        for i in range(c // 8):
            tiles.append(bw[8 * i:8 * i + 8, :] + run)
            run = run + bw[8 * i + 7:8 * i + 8, :]
        b = jnp.concatenate(tiles, axis=0)

        a = jnp.zeros((c, c), F32)
        for k, m in enumerate(levels):
            piv = jnp.concatenate(
                [jnp.broadcast_to(b[blk * 2 * m + m - 1: blk * 2 * m + m, :], (2 * m, HG_DIM))
                 for blk in range(c // (2 * m))], axis=0)
            x = jnp.exp2(-jnp.abs(b - piv))
            am = lax.dot_general((q * x).astype(BF16), (kc * x).astype(BF16), NT_DIMS,
                                 preferred_element_type=F32)
            a = a + am * mk_ref[n_scan + k]

        gd = None
        for d in range(HG_DIAG):
            if d == 0:
                xd = q * kc
            else:
                fd = f if d == 1 else pltpu.roll(f, d - 1, axis=0)
                gd = fd if gd is None else gd * fd
                xd = q * pltpu.roll(kc, d, axis=0) * gd
            a = a + jnp.sum(xd, axis=-1, keepdims=True) * mk_ref[n_scan + len(levels) + d]

        qd = (q * jnp.exp2(b)).astype(BF16)
        o = lax.dot_general(qd, st.astype(BF16), NT_DIMS, preferred_element_type=F32)
        o = o + jnp.dot(a.astype(BF16), v, preferred_element_type=F32)

        b_last = b[c - 1:c, :]
        kd = (kc * jnp.exp2(b_last - b)).astype(BF16)
        st = st * jnp.exp2(b_last) + lax.dot_general(v, kd, TN_DIMS, preferred_element_type=F32)

        y = _rms_rows(o, nw_ref[...]) * (gz * _sigmoid(gz))
        o_ref[rows, cols] = y.astype(o_ref.dtype)
        return st

    sts = [st_ref[hh] for hh in range(HG_HPS)]
    for ci in range(q_ref.shape[0] // c):
        for hh in range(HG_HPS):
            sts[hh] = chunk(ci, hh, sts[hh])
    for hh in range(HG_HPS):
        st_ref[hh] = sts[hh]


def _hgrn(z3, lb, nw):
    bsz, tp, _ = z3.shape
    masks = jnp.asarray(_hgrn_masks())

    hw = HG_HPS * HG_DIM
    hgroups = HG_HEADS // HG_HPS

    def zspec(group):
        return pl.BlockSpec((None, HG_STEP, hw), lambda b, h, s: (b, s, group * hgroups + h))

    return pl.pallas_call(
        _hgrn_kernel,
        grid=(bsz, hgroups, tp // HG_STEP),
        in_specs=[zspec(0), zspec(1), zspec(2), zspec(3),
                  pl.BlockSpec((1, hw), lambda b, h, s: (0, h)),
                  pl.BlockSpec((1, HG_DIM), lambda b, h, s: (0, 0)),
                  pl.BlockSpec(masks.shape, lambda b, h, s: (0, 0, 0))],
        out_specs=pl.BlockSpec((None, HG_STEP, hw), lambda b, h, s: (b, s, h)),
        out_shape=jax.ShapeDtypeStruct((bsz, tp, HG_HEADS * HG_DIM), BF16),
        scratch_shapes=[pltpu.VMEM((HG_HPS, HG_DIM, HG_DIM), F32)],
        compiler_params=_cparams(("parallel", "parallel", "arbitrary")),
        name="hgrn2",
    )(z3, z3, z3, z3, lb, nw, masks)


def _causal_chunks(q0, tq):
    return jnp.maximum(q0 + tq - 1 - META0, 0) // DSA_KC + 1


def _chunk_row(ci, tp):
    return pl.multiple_of(jnp.minimum(META0 + ci * DSA_KC, tp - DSA_KC), 16)


def _chunked_keys(a):
    tp = a.shape[1]
    n_full = (tp - META0) // DSA_KC
    return jnp.concatenate([a[:, META0:META0 + n_full * DSA_KC], a[:, tp - DSA_KC:]], axis=1)


def _dsa_index_kernel(iq_ref, wt_ref, kt_ref, kb_ref, bias_ref, keys_ref, hi_ref, lo_ref, acc_ref, *, topk):
    tq = iq_ref.shape[0]
    kc = DSA_KC
    tp = kt_ref.shape[0]
    q0 = pl.program_id(1) * tq
    nchunks = _causal_chunks(q0, tq)
    qpos = q0 - META0 + lax.broadcasted_iota(jnp.int32, (1, tq), 1)
    int_min = jnp.int32(INT_MIN)
    half = jnp.int32(HALF_OFF)

    def score_chunk(ci, carry):
        k0 = pl.multiple_of(ci * kc, kc)
        r0 = _chunk_row(ci, tp)
        kt = kt_ref[pl.ds(r0, kc), :]
        kb = kb_ref[pl.ds(r0, kc), :]
        for p in range(IDX_HEADS // 2):
            qp = iq_ref[:, p * LANES:(p + 1) * LANES]
            st = lax.dot_general(kt, qp, NT_DIMS, preferred_element_type=F32)
            sb = lax.dot_general(kb, qp, NT_DIMS, preferred_element_type=F32)
            part = (wt_ref[2 * p:2 * p + 1, :] * jnp.maximum(st, 0.0)
                    + wt_ref[2 * p + 1:2 * p + 2, :] * jnp.maximum(sb, 0.0))
            if p == 0:
                acc_ref[...] = part
            else:
                acc_ref[...] += part
        score = acc_ref[...] + 0.0
        bits = lax.bitcast_convert_type(score, jnp.int32)
        key = bits ^ ((bits >> 31) & jnp.int32(0x7FFFFFFF))
        kpos = r0 - META0 + lax.broadcasted_iota(jnp.int32, (kc, 1), 0)
        valid = (kpos >= k0) & (kpos <= qpos)
        key = jnp.where(valid, key, int_min)
        keys_ref[pl.ds(k0, kc), :] = key
        hi_ref[pl.ds(k0, kc), :] = (key >> 16).astype(jnp.int16)
        lo_ref[pl.ds(k0, kc), :] = (((key << 16) ^ int_min) >> 16).astype(jnp.int16)
        return carry

    lax.fori_loop(0, nchunks, score_chunk, 0)

    def count_ge(src_ref, cand_u):
        cand = (cand_u - half).astype(jnp.int16)

        def count_chunk(ci, cnt):
            k0 = pl.multiple_of(ci * kc, kc)
            for r in range(kc // COUNT_ROWS):
                ks = src_ref[pl.ds(k0 + r * COUNT_ROWS, COUNT_ROWS), :]
                cnt = cnt + jnp.where(ks >= cand, jnp.int16(1), jnp.int16(0))
            return cnt

        cnt = lax.fori_loop(0, nchunks, count_chunk, jnp.zeros((COUNT_ROWS, tq), jnp.int16))
        return jnp.sum(cnt.astype(F32), axis=0, keepdims=True)

    def bisect16(src_ref, need, count_at_zero):
        def bit_pass(i, carry):
            u, at_u, above_u = carry
            cand_u = u | lax.shift_left(jnp.int32(1), 15 - i)
            c = count_ge(src_ref, cand_u)
            ok = c >= need
            return jnp.where(ok, cand_u, u), jnp.where(ok, c, at_u), jnp.where(ok, above_u, c)
        init = (jnp.zeros((1, tq), jnp.int32), count_at_zero, jnp.zeros((1, tq), F32))
        return lax.fori_loop(0, 16, bit_pass, init)

    kf = jnp.full((1, tq), float(topk), F32)
    u_hi, n_hi, above = bisect16(hi_ref, kf, kf)
    hi_thr = (u_hi - half).astype(jnp.int16)

    def narrow_chunk(ci, carry):
        rows = pl.ds(pl.multiple_of(ci * kc, kc), kc)
        lo_ref[rows, :] = jnp.where(hi_ref[rows, :] == hi_thr, lo_ref[rows, :], jnp.int16(-HALF_OFF))
        return carry

    lax.fori_loop(0, nchunks, narrow_chunk, 0)
    u_lo, n_lo, _ = bisect16(lo_ref, topk - above, n_hi - above)
    few = u_hi == 0
    thr = jnp.where(few, int_min + 1, ((u_hi - half) << 16) | u_lo)
    tied = jnp.where(few, 0.0, above + n_lo - topk)
    any_tied = jnp.max(tied) > 0.0

    @pl.when(jnp.logical_not(any_tied))
    def _():
        def write_chunk(ci, carry):
            k0 = pl.multiple_of(ci * kc, kc)
            ks = keys_ref[pl.ds(k0, kc), :]
            bias_ref[pl.ds(k0, kc), :] = jnp.where(ks >= thr, 0.0, NEG_BIG).astype(bias_ref.dtype)
            return carry

        lax.fori_loop(0, nchunks, write_chunk, 0)

    @pl.when(any_tied)
    def _():
        def count_gt(ci, cnt):
            k0 = pl.multiple_of(ci * kc, kc)
            return cnt + jnp.sum(jnp.where(keys_ref[pl.ds(k0, kc), :] > thr, 1.0, 0.0), axis=0, keepdims=True)

        room = topk - lax.fori_loop(0, nchunks, count_gt, jnp.zeros((1, tq), F32))
        lower_tri = (lax.broadcasted_iota(jnp.int32, (kc, kc), 0)
                     >= lax.broadcasted_iota(jnp.int32, (kc, kc), 1)).astype(BF16)

        def write_chunk(ci, seen):
            k0 = pl.multiple_of(ci * kc, kc)
            ks = keys_ref[pl.ds(k0, kc), :]
            eq = ks == thr
            rank = seen + jnp.dot(lower_tri, jnp.where(eq, 1.0, 0.0).astype(BF16), preferred_element_type=F32)
            sel = (ks > thr) | (eq & (rank <= room))
            bias_ref[pl.ds(k0, kc), :] = jnp.where(sel, 0.0, NEG_BIG).astype(bias_ref.dtype)
            return rank[kc - 1:kc, :]

        lax.fori_loop(0, nchunks, write_chunk, jnp.zeros((1, tq), F32))

    def fill_chunk(ci, carry):
        k0 = pl.multiple_of(ci * kc, kc)
        bias_ref[pl.ds(k0, kc), :] = jnp.full((kc, tq), NEG_BIG, bias_ref.dtype)
        return carry

    lax.fori_loop(nchunks, tp // kc, fill_chunk, 0)


def _dsa_index(z3, wt, kt, kb, topk):
    bsz, tp, _ = z3.shape
    iq_col = (HG_HEADS * HG_DIM * 4 + AT_HEADS * AT_HDIM * 3) // (IDX_HEADS * IDX_HDIM)
    return pl.pallas_call(
        functools.partial(_dsa_index_kernel, topk=topk),
        grid=(bsz, tp // DSA_TQ),
        in_specs=[pl.BlockSpec((None, DSA_TQ, IDX_HEADS * IDX_HDIM), lambda b, i: (b, i, iq_col)),
                  pl.BlockSpec((None, IDX_HEADS, DSA_TQ), lambda b, i: (b, 0, i)),
                  pl.BlockSpec((None, tp, LANES), lambda b, i: (b, 0, 0)),
                  pl.BlockSpec((None, tp, LANES), lambda b, i: (b, 0, 0))],
        out_specs=pl.BlockSpec((None, None, tp, DSA_TQ), lambda b, i: (b, i, 0, 0)),
        out_shape=jax.ShapeDtypeStruct((bsz, tp // DSA_TQ, tp, DSA_TQ), BF16),
        scratch_shapes=[pltpu.VMEM((tp, DSA_TQ), jnp.int32), pltpu.VMEM((tp, DSA_TQ), jnp.int16),
                        pltpu.VMEM((tp, DSA_TQ), jnp.int16), pltpu.VMEM((DSA_KC, DSA_TQ), F32)],
        compiler_params=_cparams(("parallel", "arbitrary")),
        name="dsa_index",
    )(z3, wt, kt, kb)


def _dsa_attn_kernel(aq_ref, k_ref, vt_ref, bias_ref, o_ref, qs_ref, acc_ref, lg_ref, p_ref):
    tq = aq_ref.shape[0]
    kc = DSA_KC
    heads = aq_ref.shape[1] // AT_HDIM
    q0 = pl.program_id(2) * tq
    nchunks = _causal_chunks(q0, tq)

    qs_ref[...] = (aq_ref[...].astype(F32) * (AT_HDIM ** -0.5 * math.log2(math.e))).astype(qs_ref.dtype)
    acc_ref[...] = jnp.zeros(acc_ref.shape, F32)
    ones_rows = (lax.broadcasted_iota(jnp.int32, (ACC_ROWS - AT_HDIM, kc), 0) == 0).astype(BF16)

    def chunk(ci, ms):
        k0 = pl.multiple_of(ci * kc, kc)
        r0 = _chunk_row(ci, k_ref.shape[0])
        bias = bias_ref[pl.ds(k0, kc), :]
        new_ms = []
        for h in range(heads):
            hs = slice(h * AT_HDIM, (h + 1) * AT_HDIM)
            lg = lax.dot_general(k_ref[pl.ds(r0, kc), hs], qs_ref[:, hs], NT_DIMS,
                                 preferred_element_type=F32).astype(BF16) + bias
            lg_ref[h] = lg
            new_ms.append(jnp.maximum(ms[h], _rows_reduce(lg, jnp.max)))
        for h in range(heads):
            hs = slice(h * AT_HDIM, (h + 1) * AT_HDIM)
            alpha = jnp.exp2(ms[h].astype(F32) - new_ms[h].astype(F32))
            p_ref[h] = jnp.exp2(lg_ref[h] - new_ms[h])
            v_ext = jnp.concatenate([vt_ref[ci, hs, :], ones_rows], axis=0)
            acc_ref[h] = alpha * acc_ref[h] + jnp.dot(v_ext, p_ref[h], preferred_element_type=F32)
        return tuple(new_ms)

    init = tuple(jnp.full((1, tq), NEG_BIG, BF16) for _ in range(heads))
    lax.fori_loop(0, nchunks, chunk, init)

    live = (q0 + lax.broadcasted_iota(jnp.int32, (1, tq), 1)) >= META0
    for h in range(heads):
        num = acc_ref[h, 0:AT_HDIM, :]
        den = acc_ref[h, AT_HDIM:AT_HDIM + 1, :]
        out_t = jnp.where(live, num / den, 0.0)
        o_ref[:, h * AT_HDIM:(h + 1) * AT_HDIM] = out_t.T.astype(o_ref.dtype)


def _dsa_attn(z3, vt, bias):
    bsz, tp, _ = z3.shape
    gw = AT_HEADS * AT_HDIM // ATT_GROUPS
    aq_col = HG_HEADS * HG_DIM * 4 // gw
    ak_col = aq_col + ATT_GROUPS
    hpg = AT_HEADS // ATT_GROUPS
    return pl.pallas_call(
        _dsa_attn_kernel,
        grid=(bsz, ATT_GROUPS, tp // DSA_TQ),
        in_specs=[pl.BlockSpec((None, DSA_TQ, gw), lambda b, g, i: (b, i, aq_col + g)),
                  pl.BlockSpec((None, tp, gw), lambda b, g, i: (b, 0, ak_col + g), pipeline_mode=pl.Buffered(1)),
                  pl.BlockSpec((None, tp // DSA_KC, gw, DSA_KC), lambda b, g, i: (b, 0, g, 0),
                               pipeline_mode=pl.Buffered(1)),
                  pl.BlockSpec((None, None, tp, DSA_TQ), lambda b, g, i: (b, i, 0, 0))],
        out_specs=pl.BlockSpec((None, DSA_TQ, gw), lambda b, g, i: (b, i, g)),
        out_shape=jax.ShapeDtypeStruct((bsz, tp, AT_HEADS * AT_HDIM), BF16),
        scratch_shapes=[pltpu.VMEM((DSA_TQ, gw), BF16),
                        pltpu.VMEM((hpg, ACC_ROWS, DSA_TQ), F32),
                        pltpu.VMEM((hpg, DSA_KC, DSA_TQ), BF16),
                        pltpu.VMEM((hpg, DSA_KC, DSA_TQ), BF16)],
        compiler_params=_cparams(("parallel", "parallel", "arbitrary")),
        name="dsa_attn",
    )(z3, z3, vt, bias)


def _merge_kernel(yh_ref, ya_ref, ga_ref, gb_ref, wh_ref, wa_ref, o_ref):
    mh = jnp.dot(yh_ref[...], wh_ref[...], preferred_element_type=F32)
    ma = jnp.dot(ya_ref[...], wa_ref[...], preferred_element_type=F32)
    o = jax.nn.sigmoid(ga_ref[...].astype(F32)) * mh + jax.nn.sigmoid(gb_ref[...].astype(F32)) * ma
    o_ref[...] = o.astype(o_ref.dtype)


def _merge(yh, ya, z, wh, wa, tm, tn):
    m, kdim = yh.shape
    n = wh.shape[1]
    ga_col = (z.shape[1] - 2 * n) // tn
    gb_col = (z.shape[1] - n) // tn
    return pl.pallas_call(
        _merge_kernel,
        grid=(m // tm, n // tn),
        in_specs=[pl.BlockSpec((tm, kdim), lambda i, j: (i, 0)),
                  pl.BlockSpec((tm, kdim), lambda i, j: (i, 0)),
                  pl.BlockSpec((tm, tn), lambda i, j: (i, ga_col + j)),
                  pl.BlockSpec((tm, tn), lambda i, j: (i, gb_col + j)),
                  pl.BlockSpec((kdim, tn), lambda i, j: (0, j)),
                  pl.BlockSpec((kdim, tn), lambda i, j: (0, j))],
        out_specs=pl.BlockSpec((tm, tn), lambda i, j: (i, j)),
        out_shape=jax.ShapeDtypeStruct((m, n), BF16),
        compiler_params=_cparams(("parallel", "arbitrary")),
        name="branch_merge",
    )(yh, ya, z, z, wh, wa)


def _out_proj_kernel(x_ref, pre_ref, m_ref, w_ref, nw_ref, h1_ref, u2_ref, *, bps):
    is_prefix = pl.program_id(0) % bps == 0
    h = jnp.where(is_prefix, pre_ref[...], x_ref[...])
    h1 = h + jnp.dot(m_ref[...], w_ref[...], preferred_element_type=F32)
    h1_ref[...] = h1
    u2_ref[...] = _rms_rows(h1, nw_ref[...]).astype(u2_ref.dtype)


def _out_proj(x2, prefix, mm, w, nw, bsz, tp):
    m, d = mm.shape
    tm = SEQ_OFF
    bps = tp // tm
    row = lambda r: (r, 0)
    fixed = lambda r: (0, 0)
    return pl.pallas_call(
        functools.partial(_out_proj_kernel, bps=bps),
        grid=(m // tm,),
        in_specs=[pl.BlockSpec((tm, d), lambda r: (_x_block(r, bps), 0)), pl.BlockSpec((tm, d), fixed),
                  pl.BlockSpec((tm, d), row), pl.BlockSpec((d, d), fixed), pl.BlockSpec((1, d), fixed)],
        out_specs=[pl.BlockSpec((tm, d), row), pl.BlockSpec((tm, d), row)],
        out_shape=[jax.ShapeDtypeStruct((m, d), F32), jax.ShapeDtypeStruct((m, d), BF16)],
        compiler_params=_cparams(("parallel",)),
        name="out_proj",
    )(x2, prefix, mm, w, nw)


def _ffn_up_kernel(u_ref, wg_ref, wu_ref, cw_ref, cb_ref, o_ref, tail_ref, wg_scr, wu_scr):
    tm = u_ref.shape[0]

    @pl.when(pl.program_id(1) == 0)
    def _():
        tail_ref[...] = jnp.zeros_like(tail_ref)
        wg_scr[...] = wg_ref[...].astype(BF16)
        wu_scr[...] = wu_ref[...].astype(BF16)

    u = u_ref[...]
    g0 = jnp.dot(u, wg_scr[...], preferred_element_type=F32)
    up = jnp.dot(u, wu_scr[...], preferred_element_type=F32)
    row = lax.broadcasted_iota(jnp.int32, (tm, 1), 0)
    prev1 = tail_ref[7:8, :]
    prev2 = tail_ref[6:7, :]
    g1 = jnp.where(row == 0, prev1, pltpu.roll(g0, 1, axis=0))
    g2 = jnp.where(row == 0, prev2, jnp.where(row == 1, prev1, pltpu.roll(g0, 2, axis=0)))
    tail_ref[...] = g0[tm - 8:tm, :]
    a = cw_ref[0:1, :] * g2 + cw_ref[1:2, :] * g1 + cw_ref[2:3, :] * g0 + cb_ref[...]
    o_ref[...] = (a * _sigmoid(a) * up).astype(o_ref.dtype)


def _ffn_up(u2, wg, wu, cw, cb, tm, tn):
    m, d = u2.shape
    n = wg.shape[1]
    return pl.pallas_call(
        _ffn_up_kernel,
        grid=(n // tn, m // tm),
        in_specs=[pl.BlockSpec((tm, d), lambda j, i: (i, 0)),
                  pl.BlockSpec((d, tn), lambda j, i: (0, j)),
                  pl.BlockSpec((d, tn), lambda j, i: (0, j)),
                  pl.BlockSpec((8, tn), lambda j, i: (0, j)),
                  pl.BlockSpec((1, tn), lambda j, i: (0, j))],
        out_specs=pl.BlockSpec((tm, tn), lambda j, i: (i, j)),
        out_shape=jax.ShapeDtypeStruct((m, n), BF16),
        scratch_shapes=[pltpu.VMEM((8, tn), F32), pltpu.VMEM((d, tn), BF16), pltpu.VMEM((d, tn), BF16)],
        compiler_params=_cparams(("parallel", "arbitrary")),
        name="ffn_up",
    )(u2, wg, wu, cw, cb)


def _ffn_down_kernel(g_ref, wd_ref, h1_ref, nw_ref, o_ref):
    h2 = h1_ref[...] + jnp.dot(g_ref[...], wd_ref[...], preferred_element_type=F32)
    o_ref[...] = _rms_rows(h2, nw_ref[...])


def _ffn_down(gact, wd, h1, nw, bsz, seq, tp, tm):
    m, kdim = gact.shape
    d = wd.shape[1]
    per_seq = seq // tm
    off = SEQ_OFF // tm

    def row_in(i):
        return ((i // per_seq) * (tp // tm) + off + i % per_seq, 0)

    return pl.pallas_call(
        _ffn_down_kernel,
        grid=(bsz * per_seq,),
        in_specs=[pl.BlockSpec((tm, kdim), row_in),
                  pl.BlockSpec((kdim, d), lambda i: (0, 0), pipeline_mode=pl.Buffered(1)),
                  pl.BlockSpec((tm, d), row_in),
                  pl.BlockSpec((1, d), lambda i: (0, 0))],
        out_specs=pl.BlockSpec((tm, d), lambda i: (i, 0)),
        out_shape=jax.ShapeDtypeStruct((bsz * seq, d), F32),
        compiler_params=_cparams(("parallel",)),
        name="ffn_down",
    )(gact, wd, h1, nw)


def kernel(x, meta_tokens, attn_norm_w, w_in, hgrn_lb_logits, hgrn_norm_w, idx_k_norm_w, w_branch_hgrn,
           w_branch_dsa, w_out, ffn_norm_w, w_ffn_gate, w_ffn_up, ffn_conv_w, ffn_conv_b, w_ffn_down,
           final_norm_w):
    bsz, seq, d = x.shape
    depth = w_in.shape[0]
    topk = min(IDX_TOPK_MAX, seq // 4)
    tp = SEQ_OFF + seq
    m = bsz * tp
    hg_w = HG_HEADS * HG_DIM
    at_w = AT_HEADS * AT_HDIM
    n_main_lo = 4 * hg_w + 3 * at_w + IDX_HEADS * IDX_HDIM
    n_small = IDX_HDIM + IDX_HEADS

    prefix = jnp.concatenate([jnp.zeros((META0, d), x.dtype), meta_tokens.astype(x.dtype)], axis=0)
    x2 = x.reshape(bsz * seq, d)
    lbs = jnp.cumsum(jax.nn.softmax(hgrn_lb_logits.astype(F32), axis=0), axis=0)

    assert depth == 1, "multi-layer stacks are not supported"
    out = None
    for l in range(depth):
        wt = jnp.swapaxes(w_in[l], 0, 1)
        w_small = jnp.pad(wt[n_main_lo:n_main_lo + n_small], ((0, LANES - n_small), (0, 0))).T.astype(BF16)
        knw = jnp.pad(idx_k_norm_w[l], (0, LANES - IDX_HDIM)).reshape(1, LANES)

        u, zs, kt, kb = _embed(x2, prefix, attn_norm_w[l].reshape(1, d), w_small, knw, bsz, tp)
        z = _in_proj(u, wt, n_main_lo, n_main_lo + n_small, 2 * d, tm=MM_ROWS, tn=MM_COLS)
        z3 = z.reshape(bsz, tp, z.shape[1])

        y_h = _hgrn(z3, lbs[l].reshape(1, hg_w), hgrn_norm_w[l].reshape(1, HG_DIM))

        wt = zs.reshape(bsz, tp, LANES)[:, :, IDX_HDIM:IDX_HDIM + IDX_HEADS]
        wt = jnp.swapaxes(wt * (IDX_HEADS ** -0.5 * IDX_HDIM ** -0.5), 1, 2)
        bias = _dsa_index(z3, wt, kt.reshape(bsz, tp, LANES), kb.reshape(bsz, tp, LANES), topk)
        av = _chunked_keys(z3[:, :, 4 * hg_w + 2 * at_w:4 * hg_w + 3 * at_w])
        vt = jnp.swapaxes(av.reshape(bsz, tp // DSA_KC, DSA_KC, at_w), 2, 3)
        y_a = _dsa_attn(z3, vt, bias)

        mm = _merge(y_h.reshape(m, hg_w), y_a.reshape(m, at_w), z,
                    w_branch_hgrn[l].astype(BF16), w_branch_dsa[l].astype(BF16), tm=MM_ROWS, tn=MM_COLS)
        h1, u2 = _out_proj(x2, prefix, mm, w_out[l].astype(BF16), ffn_norm_w[l].reshape(1, d), bsz, tp)

        cw = jnp.pad(ffn_conv_w[l], ((0, 8 - CONV_W), (0, 0)))
        gact = _ffn_up(u2, w_ffn_gate[l], w_ffn_up[l], cw,
                       ffn_conv_b[l].reshape(1, -1), tm=MM_ROWS, tn=FFN_UP_COLS)
        out = _ffn_down(gact, w_ffn_down[l].astype(BF16), h1, final_norm_w.reshape(1, d),
                        bsz, seq, tp, tm=FFN_DOWN_ROWS)
    return out.reshape(bsz, seq, d)
```
